```python
import math
import jax, jax.numpy as jnp
from jax import lax
import numpy as np

D_MODEL = 1024
BATCH = 2
SEQ = 16384
DEPTH = 2

N_EVEN = (DEPTH + 1) // 2
N_ODD = DEPTH // 2

MEM_LEN = 256
NORM_EPS = 1e-6

MLA_HEADS = 8
MLA_NOPE = 64
MLA_ROPE = 32
MLA_QK = MLA_NOPE + MLA_ROPE
MLA_V = 64
MLA_Q_RANK = 384
MLA_KV_RANK = 256
ROPE_BASE = 10000.0
Q_BLOCK = 128

RW_HEADS = 8
RW_HEAD = 64
RW_DIM = RW_HEADS * RW_HEAD
RW_DECAY_LORA = 64
RW_AAA_LORA = 64
RW_GATE_LORA = 128
RW_LN_EPS = 64e-5

MLA_IN = MLA_Q_RANK + MLA_KV_RANK + MLA_ROPE
RW_IN = 3 * RW_DIM + RW_DECAY_LORA + RW_AAA_LORA + RW_GATE_LORA
EVEN_IN = MLA_IN + RW_IN
EVEN_MIX = MLA_HEADS * MLA_V + RW_DIM

SSM_INNER = 2 * D_MODEL
SSM_HEAD = 64
SSM_HEADS = SSM_INNER // SSM_HEAD
SSM_GROUPS = 4
SSM_STATE = 128
SSM_CONV = 4
SSM_CHUNK = 256
SSM_CONV_DIM = SSM_INNER + 2 * SSM_GROUPS * SSM_STATE
ODD_IN = SSM_INNER + SSM_CONV_DIM + SSM_HEADS

X_HEADS = 4
X_HEAD = 128
X_DIM = X_HEADS * X_HEAD

FFN_HIDDEN = -((-8 * D_MODEL) // (3 * 256)) * 256

kernel_name = "hybrid_mla_rwkv7_mamba2_memxattn"


def rms_norm(x, g, eps=NORM_EPS):
    xf = x.astype(jnp.float32)
    y = xf * lax.rsqrt(jnp.mean(xf * xf, axis=-1, keepdims=True) + eps)
    return (y * g.astype(jnp.float32)).astype(x.dtype)


def apply_rope(x, positions):
    half = x.shape[-1] // 2
    inv_freq = ROPE_BASE ** (-jnp.arange(half, dtype=jnp.float32) / half)
    ang = positions.astype(jnp.float32)[:, :, None, None] * inv_freq
    cos, sin = jnp.cos(ang), jnp.sin(ang)
    xf = x.astype(jnp.float32)
    x1, x2 = xf[..., :half], xf[..., half:]
    return jnp.concatenate([x1 * cos - x2 * sin, x2 * cos + x1 * sin], axis=-1).astype(x.dtype)


def causal_block_attention(q, k, v):
    b, s, h, dk = q.shape
    nb = s // Q_BLOCK
    scale = dk ** -0.5
    qb = jnp.moveaxis(q.reshape(b, nb, Q_BLOCK, h, dk), 1, 0)
    k_idx = jnp.arange(s)

    def block(args):
        q_blk, blk = args
        scores = jnp.einsum("bqhd,bkhd->bhqk", q_blk, k, preferred_element_type=jnp.float32) * scale
        q_idx = blk * Q_BLOCK + jnp.arange(Q_BLOCK)
        scores = jnp.where(k_idx[None, :] <= q_idx[:, None], scores, -jnp.inf)
        probs = jax.nn.softmax(scores, axis=-1).astype(v.dtype)
        return jnp.einsum("bhqk,bkhd->bqhd", probs, v)

    out = lax.map(block, (qb, jnp.arange(nb)))
    return jnp.moveaxis(out, 0, 1).reshape(b, s, h, v.shape[-1])


def mla_group(p, positions, q_norm, w_uq, kv_norm, w_ukv, q_hnorm, k_hnorm):
    b, s, _ = p.shape
    c_q = rms_norm(p[..., :MLA_Q_RANK], q_norm)
    c_kv = rms_norm(p[..., MLA_Q_RANK:MLA_Q_RANK + MLA_KV_RANK], kv_norm)
    k_rope = p[..., MLA_Q_RANK + MLA_KV_RANK:]
    q = (c_q @ w_uq).reshape(b, s, MLA_HEADS, MLA_QK)
    kv = (c_kv @ w_ukv).reshape(b, s, MLA_HEADS, MLA_NOPE + MLA_V)
    k_nope, v = kv[..., :MLA_NOPE], kv[..., MLA_NOPE:]
    k = jnp.concatenate([k_nope, jnp.broadcast_to(k_rope[:, :, None, :], (b, s, MLA_HEADS, MLA_ROPE))], axis=-1)
    q = rms_norm(q, q_hnorm)
    k = rms_norm(k, k_hnorm)
    q = jnp.concatenate([q[..., :MLA_NOPE], apply_rope(q[..., MLA_NOPE:], positions)], axis=-1)
    k = jnp.concatenate([k[..., :MLA_NOPE], apply_rope(k[..., MLA_NOPE:], positions)], axis=-1)
    return causal_block_attention(q, k, v).reshape(b, s, MLA_HEADS * MLA_V)


def token_shift(p, mu):
    prev = jnp.pad(p, ((0, 0), (1, 0), (0, 0)))[:, :-1]
    return p + (prev - p) * mu


def wkv7_scan(r, w, k, v, a, bb):
    b, s, h, n = r.shape
    xs = tuple(jnp.moveaxis(t.astype(jnp.float32), 1, 0) for t in (r, w, k, v, a, bb))

    def step(state, inp):
        r_t, w_t, k_t, v_t, a_t, b_t = inp
        sa = jnp.einsum("bhvk,bhk->bhv", state, a_t)
        state = state * w_t[:, :, None, :] + sa[..., None] * b_t[:, :, None, :] + v_t[..., None] * k_t[:, :, None, :]
        return state, jnp.einsum("bhvk,bhk->bhv", state, r_t)

    state0 = jnp.zeros((b, h, n, n), jnp.float32)
    _, ys = lax.scan(step, state0, xs)
    return jnp.moveaxis(ys, 0, 1)


def rwkv7_group(p, mu, w0, w2, a0, a2, g2, k_k, k_a, r_k, ln_g, ln_b):
    b, s, _ = p.shape
    p = token_shift(p, mu)
    r = p[..., :RW_DIM]
    k = p[..., RW_DIM:2 * RW_DIM]
    v = p[..., 2 * RW_DIM:3 * RW_DIM]
    o = 3 * RW_DIM
    xw = p[..., o:o + RW_DECAY_LORA]
    o += RW_DECAY_LORA
    xa = p[..., o:o + RW_AAA_LORA]
    o += RW_AAA_LORA
    xg = p[..., o:o + RW_GATE_LORA]
    w_log = -jax.nn.softplus(-(w0 + jnp.tanh(xw) @ w2)) - 0.5
    decay = jnp.exp(-jnp.exp(w_log.astype(jnp.float32)))
    a = jax.nn.sigmoid(a0 + xa @ a2)
    g = jax.nn.sigmoid(xg) @ g2
    heads = lambda t: t.reshape(b, s, RW_HEADS, RW_HEAD)
    kk = heads(k * k_k).astype(jnp.float32)
    kk = kk / jnp.maximum(jnp.sqrt(jnp.sum(kk * kk, axis=-1, keepdims=True)), 1e-12)
    k = k * (1.0 + (a - 1.0) * k_a)
    r_h, k_h, v_h, a_h = heads(r), heads(k), heads(v), heads(a).astype(jnp.float32)
    y = wkv7_scan(r_h, heads(decay), k_h, v_h, -kk, kk * a_h)
    mean = jnp.mean(y, axis=-1, keepdims=True)
    var = jnp.mean(jnp.square(y - mean), axis=-1, keepdims=True)
    y = ((y - mean) * lax.rsqrt(var + RW_LN_EPS)).reshape(b, s, RW_DIM)
    y = (y * ln_g.astype(jnp.float32) + ln_b.astype(jnp.float32)).astype(p.dtype)
    bonus = jnp.sum(r_h * k_h * r_k, axis=-1, keepdims=True) * v_h
    y = y + bonus.reshape(b, s, RW_DIM)
    return y * g


def even_mixer(h, positions, norm, w_in, q_norm, w_uq, kv_norm, w_ukv, q_hnorm, k_hnorm,
               mu, w0, w2, a0, a2, g2, k_k, k_a, r_k, ln_g, ln_b, w_out):
    proj = rms_norm(h, norm) @ w_in
    y_mla = mla_group(proj[..., :MLA_IN], positions, q_norm, w_uq, kv_norm, w_ukv, q_hnorm, k_hnorm)
    y_rw = rwkv7_group(proj[..., MLA_IN:], mu, w0, w2, a0, a2, g2, k_k, k_a, r_k, ln_g, ln_b)
    return jnp.concatenate([y_mla, y_rw], axis=-1) @ w_out


def causal_depthwise_conv(x, w, bias):
    width, c = w.shape
    out = lax.conv_general_dilated(x, w[:, None, :].astype(x.dtype), window_strides=(1,),
                                   padding=((width - 1, 0),),
                                   dimension_numbers=("NWC", "WIO", "NWC"),
                                   feature_group_count=c)
    return out + bias


def ssd_chunked(x, a, b_in, c_in):
    bsz, s, n_heads, head_p = x.shape
    n_groups, n_state = b_in.shape[2], b_in.shape[3]
    e_per = n_heads // n_groups
    L = SSM_CHUNK
    nc = -(-s // L)
    pad = nc * L - s
    padt = lambda t: jnp.pad(t.astype(jnp.float32), ((0, 0), (0, pad)) + ((0, 0),) * (t.ndim - 2))
    xc = padt(x).reshape(bsz, nc, L, n_groups, e_per, head_p)
    ac = padt(a).reshape(bsz, nc, L, n_groups, e_per)
    bc = padt(b_in).reshape(bsz, nc, L, n_groups, n_state)
    cc = padt(c_in).reshape(bsz, nc, L, n_groups, n_state)
    xs = tuple(jnp.moveaxis(t, 1, 0) for t in (xc, ac, bc, cc))
    causal = jnp.tril(jnp.ones((L, L), dtype=bool))[None, :, :, None, None]

    def step(state, inp):
        x_c, a_c, b_c, c_c = inp
        cum = jnp.cumsum(a_c, axis=1)
        seg = cum[:, :, None] - cum[:, None, :]
        decay = jnp.exp(jnp.where(causal, seg, -jnp.inf))
        cb = jnp.einsum("blgn,bsgn->blsg", c_c, b_c)
        y = jnp.einsum("blsg,blsge,bsgep->blgep", cb, decay, x_c)
        y = y + jnp.einsum("blgn,bgepn->blgep", c_c, state) * jnp.exp(cum)[..., None]
        to_end = jnp.exp(cum[:, -1:] - cum)
        state = state * jnp.exp(cum[:, -1])[..., None, None] + jnp.einsum("blgn,blge,blgep->bgepn", b_c, to_end, x_c)
        return state, y

    state0 = jnp.zeros((bsz, n_groups, e_per, head_p, n_state), jnp.float32)
    _, ys = lax.scan(step, state0, xs)
    return jnp.moveaxis(ys, 0, 1).reshape(bsz, nc * L, n_heads, head_p)[:, :s]


def mamba2_mixer(h, norm, w_in, conv_w, conv_b, dt_bias, a_log, d_skip, gnorm, w_out):
    b, s, _ = h.shape
    proj = rms_norm(h, norm) @ w_in
    z = proj[..., :SSM_INNER]
    xbc = proj[..., SSM_INNER:SSM_INNER + SSM_CONV_DIM]
    dt_raw = proj[..., SSM_INNER + SSM_CONV_DIM:]
    xbc = jax.nn.silu(causal_depthwise_conv(xbc, conv_w, conv_b))
    gn = SSM_GROUPS * SSM_STATE
    xs = xbc[..., :SSM_INNER].reshape(b, s, SSM_HEADS, SSM_HEAD)
    b_in = xbc[..., SSM_INNER:SSM_INNER + gn].reshape(b, s, SSM_GROUPS, SSM_STATE)
    c_in = xbc[..., SSM_INNER + gn:].reshape(b, s, SSM_GROUPS, SSM_STATE)
    dt = jax.nn.softplus(dt_raw.astype(jnp.float32) + dt_bias.astype(jnp.float32))
    A = -jnp.exp(a_log.astype(jnp.float32))
    xf = xs.astype(jnp.float32)
    y = ssd_chunked(xf * dt[..., None], dt * A, b_in, c_in)
    y = y + xf * d_skip.astype(jnp.float32)[:, None]
    y = y.reshape(b, s, SSM_INNER) * jax.nn.silu(z.astype(jnp.float32))
    y = y.reshape(b, s, SSM_GROUPS, SSM_INNER // SSM_GROUPS)
    y = y * lax.rsqrt(jnp.mean(y * y, axis=-1, keepdims=True) + NORM_EPS)
    y = (y.reshape(b, s, SSM_INNER) * gnorm.astype(jnp.float32)).astype(h.dtype)
    return y @ w_out


def memory_cross_attention(h, mem, norm_x, norm_mem, wq, wkv, q_hnorm, k_hnorm, wo):
    b, s, _ = h.shape
    m = mem.shape[1]
    q = (rms_norm(h, norm_x) @ wq).reshape(b, s, X_HEADS, X_HEAD)
    kv = (rms_norm(mem, norm_mem) @ wkv).reshape(b, m, 2, X_HEADS, X_HEAD)
    q = rms_norm(q, q_hnorm)
    k = rms_norm(kv[:, :, 0], k_hnorm)
    v = kv[:, :, 1]
    scores = jnp.einsum("bqhd,bmhd->bhqm", q, k, preferred_element_type=jnp.float32) * (X_HEAD ** -0.5)
    probs = jax.nn.softmax(scores, axis=-1).astype(v.dtype)
    out = jnp.einsum("bhqm,bmhd->bqhd", probs, v).reshape(b, s, X_DIM)
    return out @ wo


def swiglu_ffn(h, norm, w13, w2):
    u = rms_norm(h, norm) @ w13
    return (jax.nn.silu(u[..., :FFN_HIDDEN]) * u[..., FFN_HIDDEN:]) @ w2


def setup_inputs(seed: int = 0) -> dict:
    key = jax.random.key(seed)
    ks = iter(jax.random.split(key, 64))
    f32 = jnp.float32

    def nrm(shape, fan_in, scale=1.0):
        return scale * fan_in ** -0.5 * jax.random.normal(next(ks), shape, f32)

    def gain(shape):
        return 1.0 + 0.05 * jax.random.normal(next(ks), shape, f32)

    def unif(shape, lo, hi):
        return jax.random.uniform(next(ks), shape, f32, lo, hi)

    E, O, L = N_EVEN, N_ODD, DEPTH
    x = jax.random.normal(next(ks), (BATCH, SEQ, D_MODEL), f32)
    mem = jax.random.normal(next(ks), (BATCH, MEM_LEN, D_MODEL), f32)
    positions = (jax.random.randint(next(ks), (BATCH, 1), 0, 1024, jnp.int32)
                 + jnp.arange(SEQ, dtype=jnp.int32)[None, :])
    dt0 = jnp.exp(unif((O, SSM_HEADS), math.log(1e-3), math.log(1e-1)))
    return {
        "x": x, "mem": mem, "positions": positions,
        "ev_norm": gain((E, D_MODEL)),
        "ev_w_in": nrm((E, D_MODEL, EVEN_IN), D_MODEL),
        "mla_q_norm": gain((E, MLA_Q_RANK)),
        "mla_w_uq": nrm((E, MLA_Q_RANK, MLA_HEADS * MLA_QK), MLA_Q_RANK),
        "mla_kv_norm": gain((E, MLA_KV_RANK)),
        "mla_w_ukv": nrm((E, MLA_KV_RANK, MLA_HEADS * (MLA_NOPE + MLA_V)), MLA_KV_RANK),
        "mla_q_hnorm": gain((E, MLA_QK)),
        "mla_k_hnorm": gain((E, MLA_QK)),
        "rw_mu": unif((E, RW_IN), 0.0, 1.0),
        "rw_w0": unif((E, RW_DIM), -6.0, -1.0),
        "rw_w2": nrm((E, RW_DECAY_LORA, RW_DIM), RW_DECAY_LORA),
        "rw_a0": 0.1 * jax.random.normal(next(ks), (E, RW_DIM), f32),
        "rw_a2": nrm((E, RW_AAA_LORA, RW_DIM), RW_AAA_LORA),
        "rw_g2": nrm((E, RW_GATE_LORA, RW_DIM), RW_GATE_LORA),
        "rw_k_k": unif((E, RW_DIM), 0.7, 1.0),
        "rw_k_a": unif((E, RW_DIM), 0.8, 1.2),
        "rw_r_k": 0.1 * jax.random.normal(next(ks), (E, RW_HEADS, RW_HEAD), f32),
        "rw_ln_g": gain((E, RW_DIM)),
        "rw_ln_b": 0.02 * jax.random.normal(next(ks), (E, RW_DIM), f32),
        "ev_w_out": nrm((E, EVEN_MIX, D_MODEL), EVEN_MIX),
        "od_norm": gain((O, D_MODEL)),
        "od_w_in": nrm((O, D_MODEL, ODD_IN), D_MODEL),
        "ssm_conv_w": nrm((O, SSM_CONV, SSM_CONV_DIM), SSM_CONV),
        "ssm_conv_b": 0.02 * jax.random.normal(next(ks), (O, SSM_CONV_DIM), f32),
        "ssm_dt_bias": dt0 + jnp.log(-jnp.expm1(-dt0)),
        "ssm_a_log": jnp.log(unif((O, SSM_HEADS), 1.0, 16.0)),
        "ssm_d": gain((O, SSM_HEADS)),
        "ssm_gnorm": gain((O, SSM_INNER)),
        "od_w_out": nrm((O, SSM_INNER, D_MODEL), SSM_INNER),
        "xa_norm_x": gain((L, D_MODEL)),
        "xa_norm_mem": gain((L, D_MODEL)),
        "xa_wq": nrm((L, D_MODEL, X_DIM), D_MODEL),
        "xa_wkv": nrm((L, D_MODEL, 2 * X_DIM), D_MODEL),
        "xa_q_hnorm": gain((L, X_HEAD)),
        "xa_k_hnorm": gain((L, X_HEAD)),
        "xa_wo": nrm((L, X_DIM, D_MODEL), X_DIM),
        "ffn_norm": gain((L, D_MODEL)),
        "ffn_w13": nrm((L, D_MODEL, 2 * FFN_HIDDEN), D_MODEL),
        "ffn_w2": nrm((L, FFN_HIDDEN, D_MODEL), FFN_HIDDEN),
    }


def reference(x, mem, positions,
              ev_norm, ev_w_in, mla_q_norm, mla_w_uq, mla_kv_norm, mla_w_ukv, mla_q_hnorm, mla_k_hnorm,
              rw_mu, rw_w0, rw_w2, rw_a0, rw_a2, rw_g2, rw_k_k, rw_k_a, rw_r_k, rw_ln_g, rw_ln_b, ev_w_out,
              od_norm, od_w_in, ssm_conv_w, ssm_conv_b, ssm_dt_bias, ssm_a_log, ssm_d, ssm_gnorm, od_w_out,
              xa_norm_x, xa_norm_mem, xa_wq, xa_wkv, xa_q_hnorm, xa_k_hnorm, xa_wo,
              ffn_norm, ffn_w13, ffn_w2):
    h = x
    for i in range(DEPTH):
        j = i // 2
        if i % 2 == 0:
            h = h + even_mixer(h, positions, ev_norm[j], ev_w_in[j], mla_q_norm[j], mla_w_uq[j],
                               mla_kv_norm[j], mla_w_ukv[j], mla_q_hnorm[j], mla_k_hnorm[j],
                               rw_mu[j], rw_w0[j], rw_w2[j], rw_a0[j], rw_a2[j], rw_g2[j],
                               rw_k_k[j], rw_k_a[j], rw_r_k[j], rw_ln_g[j], rw_ln_b[j], ev_w_out[j])
        else:
            h = h + mamba2_mixer(h, od_norm[j], od_w_in[j], ssm_conv_w[j], ssm_conv_b[j],
                                 ssm_dt_bias[j], ssm_a_log[j], ssm_d[j], ssm_gnorm[j], od_w_out[j])
        h = h + memory_cross_attention(h, mem, xa_norm_x[i], xa_norm_mem[i], xa_wq[i], xa_wkv[i],
                                       xa_q_hnorm[i], xa_k_hnorm[i], xa_wo[i])
        h = h + swiglu_ffn(h, ffn_norm[i], ffn_w13[i], ffn_w2[i])
    return h
```

```python
import functools
import math

import jax
import jax.numpy as jnp
from jax import lax
from jax.experimental import pallas as pl
from jax.experimental.pallas import tpu as pltpu

F32 = jnp.float32
BF16 = jnp.bfloat16

D_MODEL = 1024
MEM_LEN = 256
NORM_EPS = 1e-6

MLA_HEADS = 8
MLA_NOPE = 64
MLA_ROPE = 32
MLA_QK = MLA_NOPE + MLA_ROPE
MLA_V = 64
MLA_Q_RANK = 384
MLA_KV_RANK = 256
ROPE_BASE = 10000.0
MLA_IN = MLA_Q_RANK + MLA_KV_RANK + MLA_ROPE

RW_HEADS = 8
RW_HEAD = 64
RW_DIM = RW_HEADS * RW_HEAD
RW_DECAY_LORA = 64
RW_AAA_LORA = 64
RW_GATE_LORA = 128
RW_LN_EPS = 64e-5
RW_IN = 3 * RW_DIM + RW_DECAY_LORA + RW_AAA_LORA + RW_GATE_LORA
RW_CHUNK = 64

SSM_INNER = 2 * D_MODEL
SSM_HEAD = 64
SSM_HEADS = SSM_INNER // SSM_HEAD
SSM_GROUPS = 4
SSM_STATE = 128
SSM_CONV = 4
SSM_CHUNK = 256
SSM_CONV_DIM = SSM_INNER + 2 * SSM_GROUPS * SSM_STATE
SSM_GROUP_W = SSM_INNER // SSM_GROUPS

X_HEADS = 4
X_HEAD = 128
X_DIM = X_HEADS * X_HEAD

FFN_HIDDEN = -((-8 * D_MODEL) // (3 * 256)) * 256

LANES = 128
V7X_VMEM_BYTES = 64 * 1024 * 1024
VMEM_LIMIT = V7X_VMEM_BYTES - 8 * 1024 * 1024

TOKEN_TILE = 512
RW_TILE = 256
FLASH_TILE = 512
NEG_BIG = -1e30
LOG2E = 1.4426950408889634


def _cparams(*sem):
  return pltpu.CompilerParams(dimension_semantics=sem, vmem_limit_bytes=VMEM_LIMIT)


def _resident(arr):
  nd = arr.ndim
  return pl.BlockSpec(arr.shape, lambda *_: (0,) * nd, pipeline_mode=pl.Buffered(1))


def _rows(tm, width):
  return pl.BlockSpec((tm, width), lambda i: (i, 0))


def _rms(x, g, eps=NORM_EPS):
  return x * lax.rsqrt(jnp.mean(x * x, axis=-1, keepdims=True) + eps) * g


def _dot(a, b):
  return jnp.dot(a.astype(BF16), b.astype(BF16), preferred_element_type=F32)


def _dot_nt(a, b):
  return lax.dot_general(a.astype(BF16), b.astype(BF16), (((1,), (1,)), ((), ())),
                         preferred_element_type=F32)


def _dot_tn(a, b):
  return lax.dot_general(a.astype(BF16), b.astype(BF16), (((0,), (0,)), ((), ())),
                         preferred_element_type=F32)


def _split(x):
  hi = x.astype(BF16)
  lo = (x - hi.astype(F32)).astype(BF16)
  return hi, lo


def _dot_exact_lhs(m, x):
  hi, lo = _split(x)
  return (jnp.dot(m, hi, preferred_element_type=F32) +
          jnp.dot(m, lo, preferred_element_type=F32))


def _dot_exact_rhs(x, m):
  hi, lo = _split(x)
  return (jnp.dot(hi, m, preferred_element_type=F32) +
          jnp.dot(lo, m, preferred_element_type=F32))


def _dot3(a, b):
  ah, al = _split(a)
  bh, bl = _split(b)
  return (jnp.dot(ah, bh, preferred_element_type=F32) +
          jnp.dot(ah, bl, preferred_element_type=F32) +
          jnp.dot(al, bh, preferred_element_type=F32))


def _sigmoid(x):
  return 1.0 / (1.0 + jnp.exp(-x))


def _silu(x):
  return x * _sigmoid(x)


def _softplus(x):
  return jnp.maximum(x, 0.0) + jnp.log(1.0 + jnp.exp(-jnp.abs(x)))


def _even_in_kernel(h_ref, pos_ref, g_ref, wa_ref, wrw_ref, qn_ref, wq_ref, kvn_ref, wk_ref,
                    wv_ref, gq_ref, gk_ref, invf_ref, q_ref, k_ref, v_ref, prw_ref):
  xn = _rms(h_ref[...], g_ref[...]).astype(BF16)
  pa = jnp.dot(xn, wa_ref[...], preferred_element_type=F32)
  prw_ref[...] = jnp.dot(xn, wrw_ref[...], preferred_element_type=F32)
  cq = _rms(pa[:, :MLA_Q_RANK], qn_ref[...]).astype(BF16)
  ckv = _rms(pa[:, MLA_Q_RANK:MLA_Q_RANK + MLA_KV_RANK], kvn_ref[...]).astype(BF16)
  kr = pa[:, MLA_Q_RANK + MLA_KV_RANK:]

  lane = lax.broadcasted_iota(jnp.int32, (1, LANES), 1)
  ang = pos_ref[...].astype(F32) * invf_ref[...]
  cs = jnp.where(lane < MLA_NOPE, 1.0, jnp.where(lane < MLA_QK, jnp.cos(ang), jnp.sin(ang)))
  in_head = lane < MLA_QK

  zq = jnp.dot(cq, wq_ref[...], preferred_element_type=F32)
  zk = jnp.dot(ckv, wk_ref[...], preferred_element_type=F32)
  zv = jnp.dot(ckv, wv_ref[...], preferred_element_type=F32)

  def head_norm_rope(z, g_ext):
    ms = jnp.sum(jnp.where(in_head, z * z, 0.0), axis=-1, keepdims=True) * (1.0 / MLA_QK)
    zn = z * lax.rsqrt(ms + NORM_EPS) * g_ext * cs
    rot = pltpu.roll(zn, LANES - MLA_ROPE, 1)
    return jnp.where(lane < MLA_NOPE, zn, jnp.where(in_head, zn + rot, 0.0))

  q_scale = MLA_QK ** -0.5 * LOG2E
  for hh in range(MLA_HEADS):
    sl = slice(LANES * hh, LANES * (hh + 1))
    q_ref[:, sl] = (head_norm_rope(zq[:, sl], gq_ref[...]) * q_scale).astype(BF16)
    k_ref[:, sl] = head_norm_rope(zk[:, sl] + kr, gk_ref[...]).astype(BF16)
    v_ref[:, sl] = jnp.where(lane == MLA_V, 1.0, zv[:, sl]).astype(BF16)


def _rot_cols(w):
  half = MLA_ROPE // 2
  return jnp.concatenate([-w[..., half:], w[..., :half]], axis=-1)


def _swap_halves(g):
  half = MLA_ROPE // 2
  return jnp.concatenate([g[..., half:], g[..., :half]], axis=-1)


def _even_in(h, pos, norm, w_in, q_norm, w_uq, kv_norm, w_ukv, q_hnorm, k_hnorm):
  t = h.shape[0]
  tm = min(TOKEN_TILE, t)
  w_cq = w_in[:, :MLA_Q_RANK]
  w_ckv = w_in[:, MLA_Q_RANK:MLA_Q_RANK + MLA_KV_RANK]
  w_kr = w_in[:, MLA_Q_RANK + MLA_KV_RANK:MLA_IN]
  w_kr_ext = jnp.concatenate([jnp.zeros((D_MODEL, MLA_NOPE), F32), w_kr, _rot_cols(w_kr)], axis=1)
  wa = jnp.concatenate([w_cq, w_ckv, w_kr_ext], axis=1).astype(BF16)
  wrw = w_in[:, MLA_IN:].astype(BF16)

  wq3 = w_uq.reshape(MLA_Q_RANK, MLA_HEADS, MLA_QK)
  wq_ext = jnp.concatenate([wq3, _rot_cols(wq3[..., MLA_NOPE:])], axis=-1)
  wq_ext = wq_ext.reshape(MLA_Q_RANK, MLA_HEADS * LANES).astype(BF16)
  wkv3 = w_ukv.reshape(MLA_KV_RANK, MLA_HEADS, MLA_NOPE + MLA_V)
  pad = jnp.zeros((MLA_KV_RANK, MLA_HEADS, LANES - MLA_NOPE), F32)
  wk_ext = jnp.concatenate([wkv3[..., :MLA_NOPE], pad], axis=-1)
  wk_ext = wk_ext.reshape(MLA_KV_RANK, MLA_HEADS * LANES).astype(BF16)
  wv_ext = jnp.concatenate([wkv3[..., MLA_NOPE:], pad], axis=-1)
  wv_ext = wv_ext.reshape(MLA_KV_RANK, MLA_HEADS * LANES).astype(BF16)

  def g_ext(g):
    return jnp.concatenate([g, _swap_halves(g[MLA_NOPE:])]).reshape(1, LANES)

  half = MLA_ROPE // 2
  inv_freq = ROPE_BASE ** (-jnp.arange(half, dtype=F32) / half)
  invf = jnp.concatenate([jnp.zeros((MLA_NOPE,), F32)] + [inv_freq] * 4).reshape(1, LANES)

  args = (h, pos.reshape(t, 1), norm.reshape(1, -1), wa, wrw, q_norm.reshape(1, -1), wq_ext,
          kv_norm.reshape(1, -1), wk_ext, wv_ext, g_ext(q_hnorm), g_ext(k_hnorm), invf)
  in_specs = [_rows(tm, D_MODEL), _rows(tm, 1)] + [_resident(a) for a in args[2:]]
  hw = MLA_HEADS * LANES
  return pl.pallas_call(
      _even_in_kernel,
      grid=(t // tm,),
      in_specs=in_specs,
      out_specs=[_rows(tm, hw), _rows(tm, hw), _rows(tm, hw), _rows(tm, RW_IN)],
      out_shape=[jax.ShapeDtypeStruct((t, hw), BF16)] * 3 + [jax.ShapeDtypeStruct((t, RW_IN), F32)],
      compiler_params=_cparams("parallel"),
      name="even_in",
  )(*args)


def _flash_kernel(q_ref, k_ref, v_ref, o_ref, *, tile):
  qi = pl.program_id(2)
  q = q_ref[...]
  row = lax.broadcasted_iota(jnp.int32, (tile, tile), 0)
  col = lax.broadcasted_iota(jnp.int32, (tile, tile), 1)

  def step(j, carry, masked):
    m, acc = carry
    start = pl.multiple_of(j * tile, tile)
    s = _dot_nt(q, k_ref[pl.ds(start, tile), :])
    if masked:
      s = jnp.where(col <= row, s, NEG_BIG)
    m_new = jnp.maximum(m, jnp.max(s, axis=-1, keepdims=True))
    p = jnp.exp2(s - m_new).astype(BF16)
    acc = acc * jnp.exp2(m - m_new) + jnp.dot(p, v_ref[pl.ds(start, tile), :],
                                              preferred_element_type=F32)
    return m_new, acc

  carry = (jnp.full((tile, 1), NEG_BIG, F32), jnp.zeros((tile, LANES), F32))
  carry = lax.fori_loop(0, qi, lambda j, c: step(j, c, False), carry)
  _, acc = step(qi, carry, True)
  o_ref[...] = (acc / acc[:, MLA_V:MLA_V + 1]).astype(BF16)


def _flash(q, k, v, batch, seq):
  tile = min(FLASH_TILE, seq)
  nq = seq // tile
  return pl.pallas_call(
      functools.partial(_flash_kernel, tile=tile),
      grid=(batch, MLA_HEADS, nq),
      in_specs=[
          pl.BlockSpec((tile, LANES), lambda b, h, i: (b * nq + i, h)),
          pl.BlockSpec((seq, LANES), lambda b, h, i: (b, h)),
          pl.BlockSpec((seq, LANES), lambda b, h, i: (b, h)),
      ],
      out_specs=pl.BlockSpec((tile, LANES), lambda b, h, i: (b * nq + i, h)),
      out_shape=jax.ShapeDtypeStruct(q.shape, BF16),
      compiler_params=_cparams("parallel", "parallel", "arbitrary"),
      name="flash",
  )(q, k, v)


def _unit_lower_inverse(l_ab, row, col):
  eye = (row == col).astype(F32)
  same16 = (row // 16) == (col // 16)
  same32 = (row // 32) == (col // 32)
  x = jnp.where(same16, l_ab, 0.0)
  x2 = _dot3(x, x)
  x4 = _dot3(x2, x2)
  x8 = _dot3(x4, x4)
  t = eye + x
  t = t + _dot3(t, x2)
  t = t + _dot3(t, x4)
  t = t + _dot3(t, x8)
  lo = jnp.where(same32 & jnp.logical_not(same16), l_ab, 0.0)
  t = t + _dot3(t, _dot3(lo, t))
  lo = jnp.where(same32, 0.0, l_ab)
  t = t + _dot3(t, _dot3(lo, t))
  return t


def _rwkv_kernel(p_ref, mu_ref, w0_ref, w2_ref, a0_ref, a2_ref, g2_ref, kk_ref, ka_ref, rk_ref,
                 lng_ref, lnb_ref, seg_ref, cum_ref, all_ref, y_ref, prev_sc, state_sc, y_sc, *, tm):
  t_idx = pl.program_id(1)

  @pl.when(t_idx == 0)
  def _():
    prev_sc[...] = jnp.zeros_like(prev_sc)
    state_sc[...] = jnp.zeros_like(state_sc)

  p = p_ref[...]
  rowid = lax.broadcasted_iota(jnp.int32, (tm, 1), 0)
  prev = jnp.where(rowid == 0, prev_sc[7:8, :], pltpu.roll(p, 1, 0))
  prev_sc[...] = p[tm - 8:, :]
  xs = p + (prev - p) * mu_ref[...]
  r = xs[:, :RW_DIM]
  k = xs[:, RW_DIM:2 * RW_DIM]
  v = xs[:, 2 * RW_DIM:3 * RW_DIM]
  lora_in = xs[:, 3 * RW_DIM:3 * RW_DIM + LANES]
  xg = xs[:, 3 * RW_DIM + LANES:]

  w_log = -_softplus(-(w0_ref[...] + _dot(jnp.tanh(lora_in), w2_ref[...]))) - 0.5
  logw = -jnp.exp(w_log)
  a = _sigmoid(a0_ref[...] + _dot(lora_in, a2_ref[...]))
  gate = _dot(_sigmoid(xg), g2_ref[...])
  seg = seg_ref[...]
  kk = k * kk_ref[...]
  kk = kk / jnp.maximum(jnp.sqrt(_dot_exact_rhs(kk * kk, seg)), 1e-12)
  km = k * (1.0 + (a - 1.0) * ka_ref[...])
  b_s = kk * a

  g_in = _dot_exact_lhs(cum_ref[...], logw)
  g_end = _dot_exact_lhs(all_ref[...], logw)
  e_in = jnp.exp(g_in)
  e_neg = jnp.exp(-g_in)
  e_end = jnp.exp(g_end - g_in)
  at = -kk * jnp.exp(g_in - logw)
  bt = b_s * e_neg
  kt = km * e_neg
  rt = r * e_in
  bh = b_s * e_end
  kh = km * e_end
  e_tot = jnp.exp(g_end)

  lane = lax.broadcasted_iota(jnp.int32, (1, LANES), 1)
  first = lane < RW_HEAD
  row = lax.broadcasted_iota(jnp.int32, (LANES, LANES), 0)
  col = lax.broadcasted_iota(jnp.int32, (LANES, LANES), 1)
  same_head = (row // RW_CHUNK) == (col // RW_CHUNK)
  strict = same_head & (col < row)
  incl = same_head & (col <= row)

  def stack(x):
    return jnp.concatenate([jnp.where(first, x, 0.0), jnp.where(first, 0.0, x)], axis=0)

  def dup(x):
    return jnp.concatenate([x, x], axis=0)

  for pr in range(RW_HEADS // 2):
    ls = slice(LANES * pr, LANES * (pr + 1))
    state = state_sc[pr]
    for c in range(tm // RW_CHUNK):
      rs = slice(RW_CHUNK * c, RW_CHUNK * (c + 1))
      sa = stack(at[rs, ls])
      sr = stack(rt[rs, ls])
      sv = stack(v[rs, ls])
      sc = _dot_nt(jnp.concatenate([sa, sr], axis=0),
                   jnp.concatenate([dup(bt[rs, ls]), dup(kt[rs, ls])], axis=0))
      l_ab = jnp.where(strict, sc[:LANES, :LANES], 0.0)
      l_ak = jnp.where(strict, sc[:LANES, LANES:], 0.0)
      l_rb = jnp.where(incl, sc[LANES:, :LANES], 0.0)
      l_rk = jnp.where(incl, sc[LANES:, LANES:], 0.0)
      tinv = _unit_lower_inverse(l_ab, row, col)
      w1 = _dot3(tinv, sa)
      w2 = _dot3(tinv, _dot(l_ak, sv))
      u = _dot_nt(w1, state) + w2
      y = _dot_nt(sr, state) + _dot(l_rb, u) + _dot(l_rk, sv)
      y_sc[rs, ls] = y[:RW_CHUNK] + y[RW_CHUNK:]
      upd = _dot_tn(jnp.concatenate([u, sv], axis=0),
                    jnp.concatenate([stack(bh[rs, ls]), stack(kh[rs, ls])], axis=0))
      state = state * e_tot[RW_CHUNK * c:RW_CHUNK * c + 1, ls] + upd
    state_sc[pr] = state

  y = y_sc[...]
  inv_n = 1.0 / RW_HEAD
  mean = _dot_exact_rhs(y, seg) * inv_n
  d = y - mean
  var = _dot_exact_rhs(d * d, seg) * inv_n
  yn = d * lax.rsqrt(var + RW_LN_EPS) * lng_ref[...] + lnb_ref[...]
  bonus = _dot_exact_rhs(r * km * rk_ref[...], seg) * v
  y_ref[...] = ((yn + bonus) * gate).astype(BF16)


def _rwkv(prw, batch, seq, mu, w0, w2, a0, a2, g2, k_k, k_a, r_k, ln_g, ln_b):
  tm = min(RW_TILE, seq)
  nt = seq // tm
  zeros = jnp.zeros((RW_DECAY_LORA, RW_DIM), F32)
  w2e = jnp.concatenate([w2, zeros], axis=0).astype(BF16)
  a2e = jnp.concatenate([zeros, a2], axis=0).astype(BF16)
  idx = jnp.arange(RW_DIM) // RW_HEAD
  seg = (idx[:, None] == idx[None, :]).astype(BF16)
  tok = jnp.arange(tm)
  same_chunk = (tok[:, None] // RW_CHUNK) == (tok[None, :] // RW_CHUNK)
  cum = (same_chunk & (tok[None, :] <= tok[:, None])).astype(BF16)
  allc = same_chunk.astype(BF16)
  row = lambda x: x.reshape(1, -1)
  args = (prw, row(mu), row(w0), w2e, row(a0), a2e, g2.astype(BF16), row(k_k), row(k_a),
          row(r_k), row(ln_g), row(ln_b), seg, cum, allc)
  in_specs = [pl.BlockSpec((tm, RW_IN), lambda b, i: (b * nt + i, 0))]
  in_specs += [_resident(a) for a in args[1:]]
  return pl.pallas_call(
      functools.partial(_rwkv_kernel, tm=tm),
      grid=(batch, nt),
      in_specs=in_specs,
      out_specs=pl.BlockSpec((tm, RW_DIM), lambda b, i: (b * nt + i, 0)),
      out_shape=jax.ShapeDtypeStruct((batch * seq, RW_DIM), BF16),
      scratch_shapes=[pltpu.VMEM((8, RW_IN), F32),
                      pltpu.VMEM((RW_HEADS // 2, LANES, LANES), F32),
                      pltpu.VMEM((tm, RW_DIM), F32)],
      compiler_params=_cparams("parallel", "arbitrary"),
      name="rwkv",
  )(*args)


def _proj_res_kernel(*refs, n):
  res_ref, o_ref = refs[0], refs[-1]
  acc = res_ref[...]
  for i in range(n):
    acc = acc + jnp.dot(refs[1 + i][...], refs[1 + n + i][...], preferred_element_type=F32)
  o_ref[...] = acc


def _proj_res(res, xs, ws):
  t = res.shape[0]
  tm = min(TOKEN_TILE, t)
  n = len(xs)
  in_specs = [_rows(tm, D_MODEL)] + [_rows(tm, x.shape[1]) for x in xs] + [_resident(w) for w in ws]
  return pl.pallas_call(
      functools.partial(_proj_res_kernel, n=n),
      grid=(t // tm,),
      in_specs=in_specs,
      out_specs=_rows(tm, D_MODEL),
      out_shape=jax.ShapeDtypeStruct((t, D_MODEL), F32),
      compiler_params=_cparams("parallel"),
      name="proj_res",
  )(res, *xs, *ws)


def _memkv_kernel(mem_ref, g_ref, w_ref, kg_ref, k_ref, v_ref):
  xn = _rms(mem_ref[...], g_ref[...]).astype(BF16)
  kv = jnp.dot(xn, w_ref[...], preferred_element_type=F32)
  for hh in range(X_HEADS):
    sl = slice(X_HEAD * hh, X_HEAD * (hh + 1))
    k_ref[:, sl] = _rms(kv[:, sl], kg_ref[...]).astype(BF16)
  v_ref[...] = kv[:, X_DIM:].astype(BF16)


def _memkv(mem2, batch, g, wkv, k_hnorm):
  args = (mem2, g.reshape(1, -1), wkv.astype(BF16), k_hnorm.reshape(1, -1))
  return pl.pallas_call(
      _memkv_kernel,
      grid=(batch,),
      in_specs=[_rows(MEM_LEN, D_MODEL)] + [_resident(a) for a in args[1:]],
      out_specs=[_rows(MEM_LEN, X_DIM)] * 2,
      out_shape=[jax.ShapeDtypeStruct((batch * MEM_LEN, X_DIM), BF16)] * 2,
      compiler_params=_cparams("parallel"),
      name="memkv",
  )(*args)


def _xattn_kernel(h_ref, k_ref, v_ref, g_ref, wq_ref, qg_ref, wo_ref, o_ref):
  h = h_ref[...]
  xn = _rms(h, g_ref[...]).astype(BF16)
  q = jnp.dot(xn, wq_ref[...], preferred_element_type=F32)
  scale = X_HEAD ** -0.5
  outs = []
  for hh in range(X_HEADS):
    sl = slice(X_HEAD * hh, X_HEAD * (hh + 1))
    qh = _rms(q[:, sl], qg_ref[...]) * scale
    s = _dot_nt(qh, k_ref[:, sl])
    e = jnp.exp(s - jnp.max(s, axis=-1, keepdims=True))
    p = e / jnp.sum(e, axis=-1, keepdims=True)
    outs.append(jnp.dot(p.astype(BF16), v_ref[:, sl], preferred_element_type=F32))
  att = jnp.concatenate(outs, axis=-1).astype(BF16)
  o_ref[...] = h + jnp.dot(att, wo_ref[...], preferred_element_type=F32)


def _xattn(h, kmem, vmem, seq, g, wq, q_hnorm, wo):
  t = h.shape[0]
  tm = min(TOKEN_TILE, seq)
  per_batch = seq // tm
  args = (h, kmem, vmem, g.reshape(1, -1), wq.astype(BF16), q_hnorm.reshape(1, -1), wo.astype(BF16))
  mem_spec = pl.BlockSpec((MEM_LEN, X_DIM), lambda i: (i // per_batch, 0))
  return pl.pallas_call(
      _xattn_kernel,
      grid=(t // tm,),
      in_specs=[_rows(tm, D_MODEL), mem_spec, mem_spec] + [_resident(a) for a in args[3:]],
      out_specs=_rows(tm, D_MODEL),
      out_shape=jax.ShapeDtypeStruct((t, D_MODEL), F32),
      compiler_params=_cparams("parallel"),
      name="xattn",
  )(*args)


FFN_CHUNK = 256


def _ffn_kernel(h_ref, g_ref, w1_ref, w3_ref, w2_ref, o_ref):
  h = h_ref[...]
  xn = _rms(h, g_ref[...]).astype(BF16)
  acc = h
  for c in range(FFN_HIDDEN // FFN_CHUNK):
    sl = slice(FFN_CHUNK * c, FFN_CHUNK * (c + 1))
    u1 = jnp.dot(xn, w1_ref[:, sl], preferred_element_type=F32)
    u3 = jnp.dot(xn, w3_ref[:, sl], preferred_element_type=F32)
    act = (_silu(u1) * u3).astype(BF16)
    acc = acc + jnp.dot(act, w2_ref[sl, :], preferred_element_type=F32)
  o_ref[...] = acc


def _ffn(h, g, w13, w2):
  t = h.shape[0]
  tm = min(TOKEN_TILE, t)
  args = (h, g.reshape(1, -1), w13[:, :FFN_HIDDEN].astype(BF16), w13[:, FFN_HIDDEN:].astype(BF16),
          w2.astype(BF16))
  return pl.pallas_call(
      _ffn_kernel,
      grid=(t // tm,),
      in_specs=[_rows(tm, D_MODEL)] + [_resident(a) for a in args[1:]],
      out_specs=_rows(tm, D_MODEL),
      out_shape=jax.ShapeDtypeStruct((t, D_MODEL), F32),
      compiler_params=_cparams("parallel"),
      name="ffn",
  )(*args)


def _odd_in_kernel(h_ref, g_ref, wz_ref, wx_ref, wdt_ref, z_ref, xbc_ref, dt_ref):
  xn = _rms(h_ref[...], g_ref[...]).astype(BF16)
  z_ref[...] = jnp.dot(xn, wz_ref[...], preferred_element_type=F32)
  xbc_ref[...] = jnp.dot(xn, wx_ref[...], preferred_element_type=F32)
  dt_ref[...] = jnp.dot(xn, wdt_ref[...], preferred_element_type=F32)


def _odd_in(h, g, w_in):
  t = h.shape[0]
  tm = min(TOKEN_TILE, t)
  wz = w_in[:, :SSM_INNER].astype(BF16)
  wx = w_in[:, SSM_INNER:SSM_INNER + SSM_CONV_DIM].astype(BF16)
  wdt = w_in[:, SSM_INNER + SSM_CONV_DIM:]
  wdt = jnp.concatenate([wdt, jnp.zeros((D_MODEL, LANES - SSM_HEADS), F32)], axis=1).astype(BF16)
  args = (h, g.reshape(1, -1), wz, wx, wdt)
  return pl.pallas_call(
      _odd_in_kernel,
      grid=(t // tm,),
      in_specs=[_rows(tm, D_MODEL)] + [_resident(a) for a in args[1:]],
      out_specs=[_rows(tm, SSM_INNER), _rows(tm, SSM_CONV_DIM), _rows(tm, LANES)],
      out_shape=[jax.ShapeDtypeStruct((t, SSM_INNER), F32),
                 jax.ShapeDtypeStruct((t, SSM_CONV_DIM), F32),
                 jax.ShapeDtypeStruct((t, LANES), F32)],
      compiler_params=_cparams("parallel"),
      name="odd_in",
  )(*args)


def _ssd_kernel(xbc_ref, z_ref, dt_ref, cw_ref, cb_ref, dtb_ref, alog_ref, dskip_ref, gn_ref,
                exp_ref, tri_ref, y_ref, xbuf_sc, state_sc, yacc_sc, *, L):
  c_idx = pl.program_id(1)

  @pl.when(c_idx == 0)
  def _():
    xbuf_sc[0:8, :] = jnp.zeros((8, SSM_CONV_DIM), F32)
    state_sc[...] = jnp.zeros_like(state_sc)

  xbuf_sc[8:8 + L, :] = xbc_ref[...]
  conv = cb_ref[...]
  for j in range(SSM_CONV):
    conv = conv + cw_ref[j:j + 1, :] * xbuf_sc[pl.ds(8 - (SSM_CONV - 1) + j, L), :]
  xbuf_sc[0:8, :] = xbuf_sc[L:L + 8, :]
  xbc = _silu(conv)
  gn_w = SSM_GROUPS * SSM_STATE
  x = xbc[:, :SSM_INNER]
  b_all = xbc[:, SSM_INNER:SSM_INNER + gn_w]
  c_all = xbc[:, SSM_INNER + gn_w:]

  dt = _softplus(dt_ref[...] + dtb_ref[...])
  a = dt * (-jnp.exp(alog_ref[...]))
  tri = tri_ref[...]
  cum = _dot_exact_lhs(tri, a)
  cum_t = cum.T
  last = cum[L - 1:L, :]
  expand = exp_ref[...]
  dt_x = _dot_exact_rhs(dt, expand)
  ecum_x = _dot_exact_rhs(jnp.exp(cum), expand)
  toend_x = _dot_exact_rhs(jnp.exp(last - cum), expand)
  elast_x = _dot_exact_rhs(jnp.broadcast_to(jnp.exp(last), (8, LANES)), expand)[0:1, :]
  xdt = x * dt_x
  xend = xdt * toend_x

  row = lax.broadcasted_iota(jnp.int32, (L, L), 0)
  col = lax.broadcasted_iota(jnp.int32, (L, L), 1)
  causal = col <= row
  lane = lax.broadcasted_iota(jnp.int32, (1, LANES), 1)
  first = lane < SSM_HEAD

  e_per = SSM_HEADS // SSM_GROUPS
  for g in range(SSM_GROUPS):
    gs = slice(SSM_GROUP_W * g, SSM_GROUP_W * (g + 1))
    bg = b_all[:, SSM_STATE * g:SSM_STATE * (g + 1)]
    cg = c_all[:, SSM_STATE * g:SSM_STATE * (g + 1)]
    cb = _dot_nt(cg, bg)
    st = state_sc[:, gs]
    yacc_sc[:, gs] = _dot(cg, st) * ecum_x[:, gs]
    for pr in range(e_per // 2):
      ps = slice(SSM_GROUP_W * g + LANES * pr, SSM_GROUP_W * g + LANES * (pr + 1))
      xp = xdt[:, ps]
      y_pair = jnp.zeros((L, LANES), F32)
      for j in range(2):
        hd = e_per * g + 2 * pr + j
        seg = cum[:, hd:hd + 1] - cum_t[hd:hd + 1, :]
        m = (cb * jnp.exp(jnp.where(causal, seg, NEG_BIG))).astype(BF16)
        xj = jnp.where(first, xp, 0.0) if j == 0 else jnp.where(first, 0.0, xp)
        y_pair = y_pair + jnp.dot(m, xj.astype(BF16), preferred_element_type=F32)
      yacc_sc[:, ps] += y_pair
    state_sc[:, gs] = st * elast_x[:, gs] + _dot_tn(bg, xend[:, gs])

  y = yacc_sc[...] + x * dskip_ref[...]
  y = y * _silu(z_ref[...])
  for g in range(SSM_GROUPS):
    gs = slice(SSM_GROUP_W * g, SSM_GROUP_W * (g + 1))
    y_ref[:, gs] = _rms(y[:, gs], gn_ref[:, gs]).astype(BF16)


def _ssd(z, xbc, dt, batch, seq, conv_w, conv_b, dt_bias, a_log, d_skip, gnorm):
  L = min(SSM_CHUNK, seq)
  nc = seq // L
  pad = lambda x: jnp.concatenate([x, jnp.zeros((LANES - SSM_HEADS,), F32)]).reshape(1, LANES)
  head_of_lane = jnp.arange(SSM_INNER) // SSM_HEAD
  expand = (jnp.arange(LANES)[:, None] == head_of_lane[None, :]).astype(BF16)
  tok = jnp.arange(L)
  tri = (tok[None, :] <= tok[:, None]).astype(BF16)
  args = (xbc, z, dt, conv_w, conv_b.reshape(1, -1), pad(dt_bias), pad(a_log),
          jnp.repeat(d_skip, SSM_HEAD).reshape(1, -1), gnorm.reshape(1, -1), expand, tri)
  tiles = lambda w: pl.BlockSpec((L, w), lambda b, i: (b * nc + i, 0))
  return pl.pallas_call(
      functools.partial(_ssd_kernel, L=L),
      grid=(batch, nc),
      in_specs=[tiles(SSM_CONV_DIM), tiles(SSM_INNER), tiles(LANES)] + [_resident(a) for a in args[3:]],
      out_specs=tiles(SSM_INNER),
      out_shape=jax.ShapeDtypeStruct((batch * seq, SSM_INNER), BF16),
      scratch_shapes=[pltpu.VMEM((L + 8, SSM_CONV_DIM), F32),
                      pltpu.VMEM((SSM_STATE, SSM_INNER), F32),
                      pltpu.VMEM((L, SSM_INNER), F32)],
      compiler_params=_cparams("parallel", "arbitrary"),
      name="ssd",
  )(*args)


def kernel(x, mem, positions, ev_norm, ev_w_in, mla_q_norm, mla_w_uq, mla_kv_norm, mla_w_ukv, mla_q_hnorm, mla_k_hnorm, rw_mu, rw_w0, rw_w2, rw_a0, rw_a2, rw_g2, rw_k_k, rw_k_a, rw_r_k, rw_ln_g, rw_ln_b, ev_w_out, od_norm, od_w_in, ssm_conv_w, ssm_conv_b, ssm_dt_bias, ssm_a_log, ssm_d, ssm_gnorm, od_w_out, xa_norm_x, xa_norm_mem, xa_wq, xa_wkv, xa_q_hnorm, xa_k_hnorm, xa_wo, ffn_norm, ffn_w13, ffn_w2):
  batch, seq, _ = x.shape
  depth = xa_wq.shape[0]
  h = x.reshape(batch * seq, D_MODEL)
  mem2 = mem.reshape(batch * MEM_LEN, D_MODEL)
  pos = positions.reshape(batch * seq)
  for i in range(depth):
    j = i // 2
    if i % 2 == 0:
      q, k, v, prw = _even_in(h, pos, ev_norm[j], ev_w_in[j], mla_q_norm[j], mla_w_uq[j],
                              mla_kv_norm[j], mla_w_ukv[j], mla_q_hnorm[j], mla_k_hnorm[j])
      y_mla = _flash(q, k, v, batch, seq)
      y_rw = _rwkv(prw, batch, seq, rw_mu[j], rw_w0[j], rw_w2[j], rw_a0[j], rw_a2[j], rw_g2[j],
                   rw_k_k[j], rw_k_a[j], rw_r_k[j].reshape(-1), rw_ln_g[j], rw_ln_b[j])
      w_out = ev_w_out[j]
      w_mla = w_out[:MLA_HEADS * MLA_V].reshape(MLA_HEADS, MLA_V, D_MODEL)
      w_mla = jnp.concatenate([w_mla, jnp.zeros((MLA_HEADS, LANES - MLA_V, D_MODEL), F32)], axis=1)
      w_mla = w_mla.reshape(MLA_HEADS * LANES, D_MODEL).astype(BF16)
      h = _proj_res(h, [y_mla, y_rw], [w_mla, w_out[MLA_HEADS * MLA_V:].astype(BF16)])
    else:
      z, xbc, dt = _odd_in(h, od_norm[j], od_w_in[j])
      y = _ssd(z, xbc, dt, batch, seq, ssm_conv_w[j], ssm_conv_b[j], ssm_dt_bias[j], ssm_a_log[j],
               ssm_d[j], ssm_gnorm[j])
      h = _proj_res(h, [y], [od_w_out[j].astype(BF16)])
    kmem, vmem = _memkv(mem2, batch, xa_norm_mem[i], xa_wkv[i], xa_k_hnorm[i])
    h = _xattn(h, kmem, vmem, seq, xa_norm_x[i], xa_wq[i], xa_q_hnorm[i], xa_wo[i])
    h = _ffn(h, ffn_norm[i], ffn_w13[i], ffn_w2[i])
  return h.reshape(batch, seq, D_MODEL)
```

```python
import functools
import math

import jax
import jax.numpy as jnp
from jax import lax
from jax.experimental import pallas as pl
from jax.experimental.pallas import tpu as pltpu

F32 = jnp.float32
BF16 = jnp.bfloat16

D_MODEL = 1024
MEM_LEN = 256
NORM_EPS = 1e-6

MLA_HEADS = 8
MLA_NOPE = 64
MLA_ROPE = 32
MLA_QK = MLA_NOPE + MLA_ROPE
MLA_V = 64
MLA_Q_RANK = 384
MLA_KV_RANK = 256
ROPE_BASE = 10000.0
MLA_IN = MLA_Q_RANK + MLA_KV_RANK + MLA_ROPE

RW_HEADS = 8
RW_HEAD = 64
RW_DIM = RW_HEADS * RW_HEAD
RW_DECAY_LORA = 64
RW_AAA_LORA = 64
RW_GATE_LORA = 128
RW_LN_EPS = 64e-5
RW_IN = 3 * RW_DIM + RW_DECAY_LORA + RW_AAA_LORA + RW_GATE_LORA
RW_CHUNK = 64

SSM_INNER = 2 * D_MODEL
SSM_HEAD = 64
SSM_HEADS = SSM_INNER // SSM_HEAD
SSM_GROUPS = 4
SSM_STATE = 128
SSM_CONV = 4
SSM_CHUNK = 256
SSM_CONV_DIM = SSM_INNER + 2 * SSM_GROUPS * SSM_STATE
SSM_GROUP_W = SSM_INNER // SSM_GROUPS

X_HEADS = 4
X_HEAD = 128
X_DIM = X_HEADS * X_HEAD

FFN_HIDDEN = -((-8 * D_MODEL) // (3 * 256)) * 256

LANES = 128
V7X_VMEM_BYTES = 64 * 1024 * 1024
VMEM_LIMIT = V7X_VMEM_BYTES - 8 * 1024 * 1024

TOKEN_TILE = 512
RW_TILE = 256
FLASH_TILE = 1024
NEG_BIG = -1e30
LOG2E = 1.4426950408889634


def _cparams(*sem):
  return pltpu.CompilerParams(dimension_semantics=sem, vmem_limit_bytes=VMEM_LIMIT)


def _resident(arr):
  nd = arr.ndim
  return pl.BlockSpec(arr.shape, lambda *_: (0,) * nd, pipeline_mode=pl.Buffered(1))


def _rows(tm, width):
  return pl.BlockSpec((tm, width), lambda i: (i, 0))


def _rms(x, g, eps=NORM_EPS):
  return x * lax.rsqrt(jnp.mean(x * x, axis=-1, keepdims=True) + eps) * g


def _dot(a, b):
  return jnp.dot(a.astype(BF16), b.astype(BF16), preferred_element_type=F32)


def _dot_nt(a, b):
  return lax.dot_general(a.astype(BF16), b.astype(BF16), (((1,), (1,)), ((), ())),
                         preferred_element_type=F32)


def _dot_tn(a, b):
  return lax.dot_general(a.astype(BF16), b.astype(BF16), (((0,), (0,)), ((), ())),
                         preferred_element_type=F32)


def _split(x):
  hi = x.astype(BF16)
  lo = (x - hi.astype(F32)).astype(BF16)
  return hi, lo


def _dot_exact_lhs(m, x):
  hi, lo = _split(x)
  return (jnp.dot(m, hi, preferred_element_type=F32) +
          jnp.dot(m, lo, preferred_element_type=F32))


def _dot_exact_rhs(x, m):
  hi, lo = _split(x)
  return (jnp.dot(hi, m, preferred_element_type=F32) +
          jnp.dot(lo, m, preferred_element_type=F32))


def _sigmoid(x):
  return 1.0 / (1.0 + jnp.exp(-x))


def _silu(x):
  return x * _sigmoid(x)


def _softplus(x):
  return jnp.maximum(x, 0.0) + jnp.log(1.0 + jnp.exp(-jnp.abs(x)))


def _even_in_kernel(h_ref, pos_ref, g_ref, wa_ref, wrw_ref, qn_ref, wq_ref, kvn_ref, wk_ref,
                    wv_ref, gq_ref, gk_ref, invf_ref, q_ref, k_ref, v_ref, prw_ref):
  xn = _rms(h_ref[...], g_ref[...]).astype(BF16)
  pa = jnp.dot(xn, wa_ref[...], preferred_element_type=F32)
  prw_ref[...] = jnp.dot(xn, wrw_ref[...], preferred_element_type=F32)
  cq = _rms(pa[:, :MLA_Q_RANK], qn_ref[...]).astype(BF16)
  ckv = _rms(pa[:, MLA_Q_RANK:MLA_Q_RANK + MLA_KV_RANK], kvn_ref[...]).astype(BF16)
  kr = pa[:, MLA_Q_RANK + MLA_KV_RANK:]

  lane = lax.broadcasted_iota(jnp.int32, (1, LANES), 1)
  ang = pos_ref[...].astype(F32) * invf_ref[...]
  cs = jnp.where(lane < MLA_NOPE, 1.0, jnp.where(lane < MLA_QK, jnp.cos(ang), jnp.sin(ang)))
  in_head = lane < MLA_QK

  zq = jnp.dot(cq, wq_ref[...], preferred_element_type=F32)
  zk = jnp.dot(ckv, wk_ref[...], preferred_element_type=F32)
  zv = jnp.dot(ckv, wv_ref[...], preferred_element_type=F32)

  def head_norm_rope(z, g_ext):
    ms = jnp.sum(jnp.where(in_head, z * z, 0.0), axis=-1, keepdims=True) * (1.0 / MLA_QK)
    zn = z * lax.rsqrt(ms + NORM_EPS) * g_ext * cs
    rot = pltpu.roll(zn, LANES - MLA_ROPE, 1)
    return jnp.where(lane < MLA_NOPE, zn, jnp.where(in_head, zn + rot, 0.0))

  q_scale = MLA_QK ** -0.5 * LOG2E
  for hh in range(MLA_HEADS):
    sl = slice(LANES * hh, LANES * (hh + 1))
    q_ref[:, sl] = (head_norm_rope(zq[:, sl], gq_ref[...]) * q_scale).astype(BF16)
    k_ref[:, sl] = head_norm_rope(zk[:, sl] + kr, gk_ref[...]).astype(BF16)
    v_ref[:, sl] = jnp.where(lane == MLA_V, 1.0, zv[:, sl]).astype(BF16)


def _rot_cols(w):
  half = MLA_ROPE // 2
  return jnp.concatenate([-w[..., half:], w[..., :half]], axis=-1)


def _swap_halves(g):
  half = MLA_ROPE // 2
  return jnp.concatenate([g[..., half:], g[..., :half]], axis=-1)


def _even_in(h, pos, norm, w_in, q_norm, w_uq, kv_norm, w_ukv, q_hnorm, k_hnorm):
  t = h.shape[0]
  tm = min(TOKEN_TILE, t)
  w_cq = w_in[:, :MLA_Q_RANK]
  w_ckv = w_in[:, MLA_Q_RANK:MLA_Q_RANK + MLA_KV_RANK]
  w_kr = w_in[:, MLA_Q_RANK + MLA_KV_RANK:MLA_IN]
  w_kr_ext = jnp.concatenate([jnp.zeros((D_MODEL, MLA_NOPE), F32), w_kr, _rot_cols(w_kr)], axis=1)
  wa = jnp.concatenate([w_cq, w_ckv, w_kr_ext], axis=1).astype(BF16)
  wrw = w_in[:, MLA_IN:].astype(BF16)

  wq3 = w_uq.reshape(MLA_Q_RANK, MLA_HEADS, MLA_QK)
  wq_ext = jnp.concatenate([wq3, _rot_cols(wq3[..., MLA_NOPE:])], axis=-1)
  wq_ext = wq_ext.reshape(MLA_Q_RANK, MLA_HEADS * LANES).astype(BF16)
  wkv3 = w_ukv.reshape(MLA_KV_RANK, MLA_HEADS, MLA_NOPE + MLA_V)
  pad = jnp.zeros((MLA_KV_RANK, MLA_HEADS, LANES - MLA_NOPE), F32)
  wk_ext = jnp.concatenate([wkv3[..., :MLA_NOPE], pad], axis=-1)
  wk_ext = wk_ext.reshape(MLA_KV_RANK, MLA_HEADS * LANES).astype(BF16)
  wv_ext = jnp.concatenate([wkv3[..., MLA_NOPE:], pad], axis=-1)
  wv_ext = wv_ext.reshape(MLA_KV_RANK, MLA_HEADS * LANES).astype(BF16)

  def g_ext(g):
    return jnp.concatenate([g, _swap_halves(g[MLA_NOPE:])]).reshape(1, LANES)

  half = MLA_ROPE // 2
  inv_freq = ROPE_BASE ** (-jnp.arange(half, dtype=F32) / half)
  invf = jnp.concatenate([jnp.zeros((MLA_NOPE,), F32)] + [inv_freq] * 4).reshape(1, LANES)

  args = (h, pos.reshape(t, 1), norm.reshape(1, -1), wa, wrw, q_norm.reshape(1, -1), wq_ext,
          kv_norm.reshape(1, -1), wk_ext, wv_ext, g_ext(q_hnorm), g_ext(k_hnorm), invf)
  in_specs = [_rows(tm, D_MODEL), _rows(tm, 1)] + [_resident(a) for a in args[2:]]
  hw = MLA_HEADS * LANES
  return pl.pallas_call(
      _even_in_kernel,
      grid=(t // tm,),
      in_specs=in_specs,
      out_specs=[_rows(tm, hw), _rows(tm, hw), _rows(tm, hw), _rows(tm, RW_IN)],
      out_shape=[jax.ShapeDtypeStruct((t, hw), BF16)] * 3 + [jax.ShapeDtypeStruct((t, RW_IN), F32)],
      compiler_params=_cparams("parallel"),
      name="even_in",
  )(*args)


def _flash_kernel(q_ref, k_ref, v_ref, o_ref, s_sc, *, tile):
  qi = pl.program_id(2)
  q = q_ref[...]
  row = lax.broadcasted_iota(jnp.int32, (tile, tile), 0)
  col = lax.broadcasted_iota(jnp.int32, (tile, tile), 1)

  def scores(j):
    return _dot_nt(q, k_ref[pl.ds(pl.multiple_of(j * tile, tile), tile), :])

  def accumulate(j, m, acc, s):
    part = s[:, :LANES]
    for c in range(1, tile // LANES):
      part = jnp.maximum(part, s[:, LANES * c:LANES * (c + 1)])
    m_new = jnp.maximum(m, jnp.max(part, axis=-1, keepdims=True))
    p = jnp.exp2(s - m_new).astype(BF16)
    pv = jnp.dot(p, v_ref[pl.ds(pl.multiple_of(j * tile, tile), tile), :],
                 preferred_element_type=F32)
    return m_new, acc * jnp.exp2(m - m_new) + pv

  def diagonal(s):
    return jnp.where(col <= row, s, NEG_BIG)

  def body(i, carry):
    m, acc = carry
    j = 2 * i
    s_sc[1] = scores(j + 1)
    m, acc = accumulate(j, m, acc, s_sc[0])
    s_sc[0] = scores(j + 2)
    return accumulate(j + 1, m, acc, s_sc[1])

  def odd_tail(m, acc):
    s_sc[1] = scores(qi)
    m, acc = accumulate(qi - 1, m, acc, s_sc[0])
    return accumulate(qi, m, acc, diagonal(s_sc[1]))

  def even_tail(m, acc):
    return accumulate(qi, m, acc, diagonal(s_sc[0]))

  s_sc[0] = scores(0)
  carry = (jnp.full((tile, 1), NEG_BIG, F32), jnp.zeros((tile, LANES), F32))
  m, acc = lax.fori_loop(0, qi // 2, body, carry)
  _, acc = lax.cond(qi % 2 == 1, odd_tail, even_tail, m, acc)
  o_ref[...] = (acc / acc[:, MLA_V:MLA_V + 1]).astype(BF16)


def _flash(q, k, v, batch, seq):
  tile = min(FLASH_TILE, seq)
  nq = seq // tile
  return pl.pallas_call(
      functools.partial(_flash_kernel, tile=tile),
      grid=(batch, MLA_HEADS, nq),
      in_specs=[
          pl.BlockSpec((tile, LANES), lambda b, h, i: (b * nq + i, h)),
          pl.BlockSpec((seq, LANES), lambda b, h, i: (b, h)),
          pl.BlockSpec((seq, LANES), lambda b, h, i: (b, h)),
      ],
      out_specs=pl.BlockSpec((tile, LANES), lambda b, h, i: (b * nq + i, h)),
      out_shape=jax.ShapeDtypeStruct(q.shape, BF16),
      scratch_shapes=[pltpu.VMEM((2, tile, tile), F32)],
      compiler_params=_cparams("parallel", "parallel", "arbitrary"),
      name="flash",
  )(q, k, v)


def _unit_lower_inverses(l_abs, row, col):
  eye = (row == col).astype(F32)
  same16 = (row // 16) == (col // 16)
  same32 = (row // 32) == (col // 32)
  off16 = same32 & jnp.logical_not(same16)
  x1 = [jnp.where(same16, l, 0.0).astype(BF16) for l in l_abs]
  x2 = [_dot(x, x).astype(BF16) for x in x1]
  x4 = [_dot(x, x).astype(BF16) for x in x2]
  x8 = [_dot(x, x).astype(BF16) for x in x4]
  ts = [eye + x.astype(F32) for x in x1]
  for xs in (x2, x4, x8):
    ts = [t + _dot(t, x) for t, x in zip(ts, xs)]
  for keep in (off16, jnp.logical_not(same32)):
    los = [jnp.where(keep, l, 0.0).astype(BF16) for l in l_abs]
    tbs = [t.astype(BF16) for t in ts]
    mids = [_dot(lo, tb) for lo, tb in zip(los, tbs)]
    ts = [t + _dot(tb, mid) for t, tb, mid in zip(ts, tbs, mids)]
  return ts


def _rwkv_kernel(p_ref, mu_ref, w0_ref, w2_ref, a0_ref, a2_ref, g2_ref, kk_ref, ka_ref, rk_ref,
                 lng_ref, lnb_ref, seg_ref, cum_ref, all_ref, y_ref, prev_sc, state_sc, y_sc, *, tm):
  t_idx = pl.program_id(1)

  @pl.when(t_idx == 0)
  def _():
    prev_sc[...] = jnp.zeros_like(prev_sc)
    state_sc[...] = jnp.zeros_like(state_sc)

  p = p_ref[...]
  rowid = lax.broadcasted_iota(jnp.int32, (tm, 1), 0)
  prev = jnp.where(rowid == 0, prev_sc[7:8, :], pltpu.roll(p, 1, 0))
  prev_sc[...] = p[tm - 8:, :]
  xs = p + (prev - p) * mu_ref[...]
  r = xs[:, :RW_DIM]
  k = xs[:, RW_DIM:2 * RW_DIM]
  v = xs[:, 2 * RW_DIM:3 * RW_DIM]
  lora_in = xs[:, 3 * RW_DIM:3 * RW_DIM + LANES]
  xg = xs[:, 3 * RW_DIM + LANES:]

  w_log = -_softplus(-(w0_ref[...] + _dot(jnp.tanh(lora_in), w2_ref[...]))) - 0.5
  logw = -jnp.exp(w_log)
  a = _sigmoid(a0_ref[...] + _dot(lora_in, a2_ref[...]))
  gate = _dot(_sigmoid(xg), g2_ref[...])
  seg = seg_ref[...]
  kk = k * kk_ref[...]
  kk = kk / jnp.maximum(jnp.sqrt(_dot_exact_rhs(kk * kk, seg)), 1e-12)
  km = k * (1.0 + (a - 1.0) * ka_ref[...])
  b_s = kk * a

  g_in = _dot_exact_lhs(cum_ref[...], logw)
  g_end = _dot_exact_lhs(all_ref[...], logw)
  e_in = jnp.exp(g_in)
  e_neg = jnp.exp(-g_in)
  e_end = jnp.exp(g_end - g_in)
  at = -kk * jnp.exp(g_in - logw)
  bt = b_s * e_neg
  kt = km * e_neg
  rt = r * e_in
  bh = b_s * e_end
  kh = km * e_end
  e_tot = jnp.exp(g_end)

  lane = lax.broadcasted_iota(jnp.int32, (1, LANES), 1)
  first = lane < RW_HEAD
  row = lax.broadcasted_iota(jnp.int32, (LANES, LANES), 0)
  col = lax.broadcasted_iota(jnp.int32, (LANES, LANES), 1)
  same_head = (row // RW_CHUNK) == (col // RW_CHUNK)
  strict = same_head & (col < row)
  incl = same_head & (col <= row)

  def stack(x):
    return jnp.concatenate([jnp.where(first, x, 0.0), jnp.where(first, 0.0, x)], axis=0)

  def dup(x):
    return jnp.concatenate([x, x], axis=0)

  n_pairs = RW_HEADS // 2
  units = [(c, pr) for c in range(tm // RW_CHUNK) for pr in range(n_pairs)]

  def tile_of(x, c, pr):
    return x[RW_CHUNK * c:RW_CHUNK * (c + 1), LANES * pr:LANES * (pr + 1)]

  sa = [stack(tile_of(at, *u)).astype(BF16) for u in units]
  sr = [stack(tile_of(rt, *u)).astype(BF16) for u in units]
  sv = [stack(tile_of(v, *u)).astype(BF16) for u in units]
  sc = [_dot_nt(jnp.concatenate([a_, r_], axis=0),
                jnp.concatenate([dup(tile_of(bt, *u)), dup(tile_of(kt, *u))], axis=0))
        for a_, r_, u in zip(sa, sr, units)]
  l_ab = [jnp.where(strict, x[:LANES, :LANES], 0.0) for x in sc]
  l_ak = [jnp.where(strict, x[:LANES, LANES:], 0.0).astype(BF16) for x in sc]
  l_rb = [jnp.where(incl, x[LANES:, :LANES], 0.0).astype(BF16) for x in sc]
  l_rk = [jnp.where(incl, x[LANES:, LANES:], 0.0).astype(BF16) for x in sc]
  tinv = [t.astype(BF16) for t in _unit_lower_inverses(l_ab, row, col)]
  w1 = [_dot(t, a_).astype(BF16) for t, a_ in zip(tinv, sa)]
  lakv = [_dot(l, v_) for l, v_ in zip(l_ak, sv)]
  w2 = [_dot(t, x) for t, x in zip(tinv, lakv)]
  yv = [_dot(l, v_) for l, v_ in zip(l_rk, sv)]

  states = [state_sc[pr] for pr in range(n_pairs)]
  for i, (c, pr) in enumerate(units):
    st = states[pr].astype(BF16)
    u = _dot_nt(w1[i], st) + w2[i]
    y = _dot_nt(sr[i], st) + _dot(l_rb[i], u) + yv[i]
    y_sc[RW_CHUNK * c:RW_CHUNK * (c + 1), LANES * pr:LANES * (pr + 1)] = y[:RW_CHUNK] + y[RW_CHUNK:]
    upd = _dot_tn(jnp.concatenate([u.astype(BF16), sv[i]], axis=0),
                  jnp.concatenate([stack(tile_of(bh, c, pr)), stack(tile_of(kh, c, pr))], axis=0))
    states[pr] = states[pr] * e_tot[RW_CHUNK * c:RW_CHUNK * c + 1, LANES * pr:LANES * (pr + 1)] + upd
  for pr in range(n_pairs):
    state_sc[pr] = states[pr]

  y = y_sc[...]
  inv_n = 1.0 / RW_HEAD
  mean = _dot_exact_rhs(y, seg) * inv_n
  d = y - mean
  var = _dot_exact_rhs(d * d, seg) * inv_n
  yn = d * lax.rsqrt(var + RW_LN_EPS) * lng_ref[...] + lnb_ref[...]
  bonus = _dot_exact_rhs(r * km * rk_ref[...], seg) * v
  y_ref[...] = ((yn + bonus) * gate).astype(BF16)


def _rwkv(prw, batch, seq, mu, w0, w2, a0, a2, g2, k_k, k_a, r_k, ln_g, ln_b):
  tm = min(RW_TILE, seq)
  nt = seq // tm
  zeros = jnp.zeros((RW_DECAY_LORA, RW_DIM), F32)
  w2e = jnp.concatenate([w2, zeros], axis=0).astype(BF16)
  a2e = jnp.concatenate([zeros, a2], axis=0).astype(BF16)
  idx = jnp.arange(RW_DIM) // RW_HEAD
  seg = (idx[:, None] == idx[None, :]).astype(BF16)
  tok = jnp.arange(tm)
  same_chunk = (tok[:, None] // RW_CHUNK) == (tok[None, :] // RW_CHUNK)
  cum = (same_chunk & (tok[None, :] <= tok[:, None])).astype(BF16)
  allc = same_chunk.astype(BF16)
  row = lambda x: x.reshape(1, -1)
  args = (prw, row(mu), row(w0), w2e, row(a0), a2e, g2.astype(BF16), row(k_k), row(k_a),
          row(r_k), row(ln_g), row(ln_b), seg, cum, allc)
  in_specs = [pl.BlockSpec((tm, RW_IN), lambda b, i: (b * nt + i, 0))]
  in_specs += [_resident(a) for a in args[1:]]
  return pl.pallas_call(
      functools.partial(_rwkv_kernel, tm=tm),
      grid=(batch, nt),
      in_specs=in_specs,
      out_specs=pl.BlockSpec((tm, RW_DIM), lambda b, i: (b * nt + i, 0)),
      out_shape=jax.ShapeDtypeStruct((batch * seq, RW_DIM), BF16),
      scratch_shapes=[pltpu.VMEM((8, RW_IN), F32),
                      pltpu.VMEM((RW_HEADS // 2, LANES, LANES), F32),
                      pltpu.VMEM((tm, RW_DIM), F32)],
      compiler_params=_cparams("parallel", "arbitrary"),
      name="rwkv",
  )(*args)


def _proj_res_kernel(*refs, n):
  res_ref, o_ref = refs[0], refs[-1]
  acc = res_ref[...]
  for i in range(n):
    acc = acc + jnp.dot(refs[1 + i][...], refs[1 + n + i][...], preferred_element_type=F32)
  o_ref[...] = acc


def _proj_res(res, xs, ws):
  t = res.shape[0]
  tm = min(TOKEN_TILE, t)
  n = len(xs)
  in_specs = [_rows(tm, D_MODEL)] + [_rows(tm, x.shape[1]) for x in xs] + [_resident(w) for w in ws]
  return pl.pallas_call(
      functools.partial(_proj_res_kernel, n=n),
      grid=(t // tm,),
      in_specs=in_specs,
      out_specs=_rows(tm, D_MODEL),
      out_shape=jax.ShapeDtypeStruct((t, D_MODEL), F32),
      compiler_params=_cparams("parallel"),
      name="proj_res",
  )(res, *xs, *ws)


def _memkv_kernel(mem_ref, g_ref, w_ref, kg_ref, k_ref, v_ref):
  xn = _rms(mem_ref[...], g_ref[...]).astype(BF16)
  kv = jnp.dot(xn, w_ref[...], preferred_element_type=F32)
  for hh in range(X_HEADS):
    sl = slice(X_HEAD * hh, X_HEAD * (hh + 1))
    k_ref[:, sl] = _rms(kv[:, sl], kg_ref[...]).astype(BF16)
  v_ref[...] = kv[:, X_DIM:].astype(BF16)


def _memkv(mem2, batch, g, wkv, k_hnorm):
  args = (mem2, g.reshape(1, -1), wkv.astype(BF16), k_hnorm.reshape(1, -1))
  return pl.pallas_call(
      _memkv_kernel,
      grid=(batch,),
      in_specs=[_rows(MEM_LEN, D_MODEL)] + [_resident(a) for a in args[1:]],
      out_specs=[_rows(MEM_LEN, X_DIM)] * 2,
      out_shape=[jax.ShapeDtypeStruct((batch * MEM_LEN, X_DIM), BF16)] * 2,
      compiler_params=_cparams("parallel"),
      name="memkv",
  )(*args)


def _xattn_kernel(h_ref, k_ref, v_ref, g_ref, wq_ref, qg_ref, wo_ref, o_ref):
  h = h_ref[...]
  xn = _rms(h, g_ref[...]).astype(BF16)
  q = jnp.dot(xn, wq_ref[...], preferred_element_type=F32)
  scale = X_HEAD ** -0.5
  outs = []
  for hh in range(X_HEADS):
    sl = slice(X_HEAD * hh, X_HEAD * (hh + 1))
    qh = _rms(q[:, sl], qg_ref[...]) * scale
    s = _dot_nt(qh, k_ref[:, sl])
    e = jnp.exp(s - jnp.max(s, axis=-1, keepdims=True))
    p = e / jnp.sum(e, axis=-1, keepdims=True)
    outs.append(jnp.dot(p.astype(BF16), v_ref[:, sl], preferred_element_type=F32))
  att = jnp.concatenate(outs, axis=-1).astype(BF16)
  o_ref[...] = h + jnp.dot(att, wo_ref[...], preferred_element_type=F32)


def _xattn(h, kmem, vmem, seq, g, wq, q_hnorm, wo):
  t = h.shape[0]
  tm = min(TOKEN_TILE, seq)
  per_batch = seq // tm
  args = (h, kmem, vmem, g.reshape(1, -1), wq.astype(BF16), q_hnorm.reshape(1, -1), wo.astype(BF16))
  mem_spec = pl.BlockSpec((MEM_LEN, X_DIM), lambda i: (i // per_batch, 0))
  return pl.pallas_call(
      _xattn_kernel,
      grid=(t // tm,),
      in_specs=[_rows(tm, D_MODEL), mem_spec, mem_spec] + [_resident(a) for a in args[3:]],
      out_specs=_rows(tm, D_MODEL),
      out_shape=jax.ShapeDtypeStruct((t, D_MODEL), F32),
      compiler_params=_cparams("parallel"),
      name="xattn",
  )(*args)


FFN_CHUNK = 256


def _ffn_kernel(h_ref, g_ref, w1_ref, w3_ref, w2_ref, o_ref):
  h = h_ref[...]
  xn = _rms(h, g_ref[...]).astype(BF16)
  acc = h
  for c in range(FFN_HIDDEN // FFN_CHUNK):
    sl = slice(FFN_CHUNK * c, FFN_CHUNK * (c + 1))
    u1 = jnp.dot(xn, w1_ref[:, sl], preferred_element_type=F32)
    u3 = jnp.dot(xn, w3_ref[:, sl], preferred_element_type=F32)
    act = (_silu(u1) * u3).astype(BF16)
    acc = acc + jnp.dot(act, w2_ref[sl, :], preferred_element_type=F32)
  o_ref[...] = acc


def _ffn(h, g, w13, w2):
  t = h.shape[0]
  tm = min(TOKEN_TILE, t)
  args = (h, g.reshape(1, -1), w13[:, :FFN_HIDDEN].astype(BF16), w13[:, FFN_HIDDEN:].astype(BF16),
          w2.astype(BF16))
  return pl.pallas_call(
      _ffn_kernel,
      grid=(t // tm,),
      in_specs=[_rows(tm, D_MODEL)] + [_resident(a) for a in args[1:]],
      out_specs=_rows(tm, D_MODEL),
      out_shape=jax.ShapeDtypeStruct((t, D_MODEL), F32),
      compiler_params=_cparams("parallel"),
      name="ffn",
  )(*args)


def _odd_in_kernel(h_ref, g_ref, wz_ref, wx_ref, wdt_ref, z_ref, xbc_ref, dt_ref):
  xn = _rms(h_ref[...], g_ref[...]).astype(BF16)
  z_ref[...] = jnp.dot(xn, wz_ref[...], preferred_element_type=F32)
  xbc_ref[...] = jnp.dot(xn, wx_ref[...], preferred_element_type=F32)
  dt_ref[...] = jnp.dot(xn, wdt_ref[...], preferred_element_type=F32)


def _odd_in(h, g, w_in):
  t = h.shape[0]
  tm = min(TOKEN_TILE, t)
  wz = w_in[:, :SSM_INNER].astype(BF16)
  wx = w_in[:, SSM_INNER:SSM_INNER + SSM_CONV_DIM].astype(BF16)
  wdt = w_in[:, SSM_INNER + SSM_CONV_DIM:]
  wdt = jnp.concatenate([wdt, jnp.zeros((D_MODEL, LANES - SSM_HEADS), F32)], axis=1).astype(BF16)
  args = (h, g.reshape(1, -1), wz, wx, wdt)
  return pl.pallas_call(
      _odd_in_kernel,
      grid=(t // tm,),
      in_specs=[_rows(tm, D_MODEL)] + [_resident(a) for a in args[1:]],
      out_specs=[_rows(tm, SSM_INNER), _rows(tm, SSM_CONV_DIM), _rows(tm, LANES)],
      out_shape=[jax.ShapeDtypeStruct((t, SSM_INNER), F32),
                 jax.ShapeDtypeStruct((t, SSM_CONV_DIM), F32),
                 jax.ShapeDtypeStruct((t, LANES), F32)],
      compiler_params=_cparams("parallel"),
      name="odd_in",
  )(*args)


def _ssd_kernel(xbc_ref, z_ref, dt_ref, cw_ref, cb_ref, dtb_ref, alog_ref, dskip_ref, gn_ref,
                exp_ref, tri_ref, y_ref, xbuf_sc, state_sc, yacc_sc, *, L):
  c_idx = pl.program_id(1)

  @pl.when(c_idx == 0)
  def _():
    xbuf_sc[0:8, :] = jnp.zeros((8, SSM_CONV_DIM), F32)
    state_sc[...] = jnp.zeros_like(state_sc)

  xbuf_sc[8:8 + L, :] = xbc_ref[...]
  conv = cb_ref[...]
  for j in range(SSM_CONV):
    conv = conv + cw_ref[j:j + 1, :] * xbuf_sc[pl.ds(8 - (SSM_CONV - 1) + j, L), :]
  xbuf_sc[0:8, :] = xbuf_sc[L:L + 8, :]
  xbc = _silu(conv)
  gn_w = SSM_GROUPS * SSM_STATE
  x = xbc[:, :SSM_INNER]
  b_all = xbc[:, SSM_INNER:SSM_INNER + gn_w]
  c_all = xbc[:, SSM_INNER + gn_w:]

  dt = _softplus(dt_ref[...] + dtb_ref[...])
  a = dt * (-jnp.exp(alog_ref[...]))
  tri = tri_ref[...]
  cum = _dot_exact_lhs(tri, a)
  cum_t = cum.T
  last = cum[L - 1:L, :]
  expand = exp_ref[...]
  dt_x = _dot_exact_rhs(dt, expand)
  ecum_x = _dot_exact_rhs(jnp.exp(cum), expand)
  toend_x = _dot_exact_rhs(jnp.exp(last - cum), expand)
  elast_x = _dot_exact_rhs(jnp.broadcast_to(jnp.exp(last), (8, LANES)), expand)[0:1, :]
  xdt = x * dt_x
  xend = xdt * toend_x

  row = lax.broadcasted_iota(jnp.int32, (L, L), 0)
  col = lax.broadcasted_iota(jnp.int32, (L, L), 1)
  causal = col <= row
  lane = lax.broadcasted_iota(jnp.int32, (1, LANES), 1)
  first = lane < SSM_HEAD

  e_per = SSM_HEADS // SSM_GROUPS
  for g in range(SSM_GROUPS):
    gs = slice(SSM_GROUP_W * g, SSM_GROUP_W * (g + 1))
    bg = b_all[:, SSM_STATE * g:SSM_STATE * (g + 1)]
    cg = c_all[:, SSM_STATE * g:SSM_STATE * (g + 1)]
    cb = _dot_nt(cg, bg)
    st = state_sc[:, gs]
    yacc_sc[:, gs] = _dot(cg, st) * ecum_x[:, gs]
    for pr in range(e_per // 2):
      ps = slice(SSM_GROUP_W * g + LANES * pr, SSM_GROUP_W * g + LANES * (pr + 1))
      xp = xdt[:, ps]
      y_pair = jnp.zeros((L, LANES), F32)
      for j in range(2):
        hd = e_per * g + 2 * pr + j
        seg = cum[:, hd:hd + 1] - cum_t[hd:hd + 1, :]
        m = (cb * jnp.exp(jnp.where(causal, seg, NEG_BIG))).astype(BF16)
        xj = jnp.where(first, xp, 0.0) if j == 0 else jnp.where(first, 0.0, xp)
        y_pair = y_pair + jnp.dot(m, xj.astype(BF16), preferred_element_type=F32)
      yacc_sc[:, ps] += y_pair
    state_sc[:, gs] = st * elast_x[:, gs] + _dot_tn(bg, xend[:, gs])

  y = yacc_sc[...] + x * dskip_ref[...]
  y = y * _silu(z_ref[...])
  for g in range(SSM_GROUPS):
    gs = slice(SSM_GROUP_W * g, SSM_GROUP_W * (g + 1))
    y_ref[:, gs] = _rms(y[:, gs], gn_ref[:, gs]).astype(BF16)


def _ssd(z, xbc, dt, batch, seq, conv_w, conv_b, dt_bias, a_log, d_skip, gnorm):
  L = min(SSM_CHUNK, seq)
  nc = seq // L
  pad = lambda x: jnp.concatenate([x, jnp.zeros((LANES - SSM_HEADS,), F32)]).reshape(1, LANES)
  head_of_lane = jnp.arange(SSM_INNER) // SSM_HEAD
  expand = (jnp.arange(LANES)[:, None] == head_of_lane[None, :]).astype(BF16)
  tok = jnp.arange(L)
  tri = (tok[None, :] <= tok[:, None]).astype(BF16)
  args = (xbc, z, dt, conv_w, conv_b.reshape(1, -1), pad(dt_bias), pad(a_log),
          jnp.repeat(d_skip, SSM_HEAD).reshape(1, -1), gnorm.reshape(1, -1), expand, tri)
  tiles = lambda w: pl.BlockSpec((L, w), lambda b, i: (b * nc + i, 0))
  return pl.pallas_call(
      functools.partial(_ssd_kernel, L=L),
      grid=(batch, nc),
      in_specs=[tiles(SSM_CONV_DIM), tiles(SSM_INNER), tiles(LANES)] + [_resident(a) for a in args[3:]],
      out_specs=tiles(SSM_INNER),
      out_shape=jax.ShapeDtypeStruct((batch * seq, SSM_INNER), BF16),
      scratch_shapes=[pltpu.VMEM((L + 8, SSM_CONV_DIM), F32),
                      pltpu.VMEM((SSM_STATE, SSM_INNER), F32),
                      pltpu.VMEM((L, SSM_INNER), F32)],
      compiler_params=_cparams("parallel", "arbitrary"),
      name="ssd",
  )(*args)


def kernel(x, mem, positions, ev_norm, ev_w_in, mla_q_norm, mla_w_uq, mla_kv_norm, mla_w_ukv, mla_q_hnorm, mla_k_hnorm, rw_mu, rw_w0, rw_w2, rw_a0, rw_a2, rw_g2, rw_k_k, rw_k_a, rw_r_k, rw_ln_g, rw_ln_b, ev_w_out, od_norm, od_w_in, ssm_conv_w, ssm_conv_b, ssm_dt_bias, ssm_a_log, ssm_d, ssm_gnorm, od_w_out, xa_norm_x, xa_norm_mem, xa_wq, xa_wkv, xa_q_hnorm, xa_k_hnorm, xa_wo, ffn_norm, ffn_w13, ffn_w2):
  batch, seq, _ = x.shape
  depth = xa_wq.shape[0]
  h = x.reshape(batch * seq, D_MODEL)
  mem2 = mem.reshape(batch * MEM_LEN, D_MODEL)
  pos = positions.reshape(batch * seq)
  for i in range(depth):
    j = i // 2
    if i % 2 == 0:
      q, k, v, prw = _even_in(h, pos, ev_norm[j], ev_w_in[j], mla_q_norm[j], mla_w_uq[j],
                              mla_kv_norm[j], mla_w_ukv[j], mla_q_hnorm[j], mla_k_hnorm[j])
      y_mla = _flash(q, k, v, batch, seq)
      y_rw = _rwkv(prw, batch, seq, rw_mu[j], rw_w0[j], rw_w2[j], rw_a0[j], rw_a2[j], rw_g2[j],
                   rw_k_k[j], rw_k_a[j], rw_r_k[j].reshape(-1), rw_ln_g[j], rw_ln_b[j])
      w_out = ev_w_out[j]
      w_mla = w_out[:MLA_HEADS * MLA_V].reshape(MLA_HEADS, MLA_V, D_MODEL)
      w_mla = jnp.concatenate([w_mla, jnp.zeros((MLA_HEADS, LANES - MLA_V, D_MODEL), F32)], axis=1)
      w_mla = w_mla.reshape(MLA_HEADS * LANES, D_MODEL).astype(BF16)
      h = _proj_res(h, [y_mla, y_rw], [w_mla, w_out[MLA_HEADS * MLA_V:].astype(BF16)])
    else:
      z, xbc, dt = _odd_in(h, od_norm[j], od_w_in[j])
      y = _ssd(z, xbc, dt, batch, seq, ssm_conv_w[j], ssm_conv_b[j], ssm_dt_bias[j], ssm_a_log[j],
               ssm_d[j], ssm_gnorm[j])
      h = _proj_res(h, [y], [od_w_out[j].astype(BF16)])
    kmem, vmem = _memkv(mem2, batch, xa_norm_mem[i], xa_wkv[i], xa_k_hnorm[i])
    h = _xattn(h, kmem, vmem, seq, xa_norm_x[i], xa_wq[i], xa_q_hnorm[i], xa_wo[i])
    h = _ffn(h, ffn_norm[i], ffn_w13[i], ffn_w2[i])
  return h.reshape(batch, seq, D_MODEL)
```

```python
import functools
import math

import jax
import jax.numpy as jnp
from jax import lax
from jax.experimental import pallas as pl
from jax.experimental.pallas import tpu as pltpu

F32 = jnp.float32
BF16 = jnp.bfloat16

D_MODEL = 1024
MEM_LEN = 256
NORM_EPS = 1e-6

MLA_HEADS = 8
MLA_NOPE = 64
MLA_ROPE = 32
MLA_QK = MLA_NOPE + MLA_ROPE
MLA_V = 64
MLA_Q_RANK = 384
MLA_KV_RANK = 256
ROPE_BASE = 10000.0
MLA_IN = MLA_Q_RANK + MLA_KV_RANK + MLA_ROPE

RW_HEADS = 8
RW_HEAD = 64
RW_DIM = RW_HEADS * RW_HEAD
RW_DECAY_LORA = 64
RW_AAA_LORA = 64
RW_GATE_LORA = 128
RW_LN_EPS = 64e-5
RW_IN = 3 * RW_DIM + RW_DECAY_LORA + RW_AAA_LORA + RW_GATE_LORA
RW_CHUNK = 64

SSM_INNER = 2 * D_MODEL
SSM_HEAD = 64
SSM_HEADS = SSM_INNER // SSM_HEAD
SSM_GROUPS = 4
SSM_STATE = 128
SSM_CONV = 4
SSM_CHUNK = 256
SSM_CONV_DIM = SSM_INNER + 2 * SSM_GROUPS * SSM_STATE
SSM_GROUP_W = SSM_INNER // SSM_GROUPS

X_HEADS = 4
X_HEAD = 128
X_DIM = X_HEADS * X_HEAD

FFN_HIDDEN = -((-8 * D_MODEL) // (3 * 256)) * 256

LANES = 128
V7X_VMEM_BYTES = 64 * 1024 * 1024
VMEM_LIMIT = V7X_VMEM_BYTES - 8 * 1024 * 1024

TOKEN_TILE = 512
RW_TILE = 128
FLASH_TILE = 1024
CONV_COLS = 512
NEG_BIG = -1e30
LOG2E = 1.4426950408889634


def _cparams(*sem):
  return pltpu.CompilerParams(dimension_semantics=sem, vmem_limit_bytes=VMEM_LIMIT)


def _resident(arr):
  nd = arr.ndim
  return pl.BlockSpec(arr.shape, lambda *_: (0,) * nd, pipeline_mode=pl.Buffered(1))


def _rows(tm, width):
  return pl.BlockSpec((tm, width), lambda i: (i, 0))


def _rms(x, g, eps=NORM_EPS):
  return x * lax.rsqrt(jnp.mean(x * x, axis=-1, keepdims=True) + eps) * g


def _dot(a, b):
  return jnp.dot(a.astype(BF16), b.astype(BF16), preferred_element_type=F32)


def _dot_nt(a, b):
  return lax.dot_general(a.astype(BF16), b.astype(BF16), (((1,), (1,)), ((), ())),
                         preferred_element_type=F32)


def _dot_tn(a, b):
  return lax.dot_general(a.astype(BF16), b.astype(BF16), (((0,), (0,)), ((), ())),
                         preferred_element_type=F32)


def _split(x):
  hi = x.astype(BF16)
  lo = (x - hi.astype(F32)).astype(BF16)
  return hi, lo


def _dot_exact_lhs(m, x):
  hi, lo = _split(x)
  return (jnp.dot(m, hi, preferred_element_type=F32) +
          jnp.dot(m, lo, preferred_element_type=F32))


def _dot_exact_rhs(x, m):
  hi, lo = _split(x)
  return (jnp.dot(hi, m, preferred_element_type=F32) +
          jnp.dot(lo, m, preferred_element_type=F32))


def _sigmoid(x):
  return 1.0 / (1.0 + jnp.exp(-x))


def _silu(x):
  return x * _sigmoid(x)


def _softplus(x):
  return jnp.maximum(x, 0.0) + jnp.log(1.0 + jnp.exp(-jnp.abs(x)))


def _even_in_kernel(h_ref, pos_ref, g_ref, wa_ref, wrw_ref, qn_ref, wq_ref, kvn_ref, wk_ref,
                    wv_ref, gq_ref, gk_ref, invf_ref, q_ref, k_ref, v_ref, prw_ref):
  xn = _rms(h_ref[...], g_ref[...]).astype(BF16)
  pa = jnp.dot(xn, wa_ref[...], preferred_element_type=F32)
  prw_ref[...] = jnp.dot(xn, wrw_ref[...], preferred_element_type=F32)
  cq = _rms(pa[:, :MLA_Q_RANK], qn_ref[...]).astype(BF16)
  ckv = _rms(pa[:, MLA_Q_RANK:MLA_Q_RANK + MLA_KV_RANK], kvn_ref[...]).astype(BF16)
  kr = pa[:, MLA_Q_RANK + MLA_KV_RANK:]

  lane = lax.broadcasted_iota(jnp.int32, (1, LANES), 1)
  ang = pos_ref[...].astype(F32) * invf_ref[...]
  cs = jnp.where(lane < MLA_NOPE, 1.0, jnp.where(lane < MLA_QK, jnp.cos(ang), jnp.sin(ang)))
  in_head = lane < MLA_QK

  zq = jnp.dot(cq, wq_ref[...], preferred_element_type=F32)
  zk = jnp.dot(ckv, wk_ref[...], preferred_element_type=F32)
  zv = jnp.dot(ckv, wv_ref[...], preferred_element_type=F32)

  def head_norm_rope(z, g_ext):
    ms = jnp.sum(jnp.where(in_head, z * z, 0.0), axis=-1, keepdims=True) * (1.0 / MLA_QK)
    zn = z * lax.rsqrt(ms + NORM_EPS) * g_ext * cs
    rot = pltpu.roll(zn, LANES - MLA_ROPE, 1)
    return jnp.where(lane < MLA_NOPE, zn, jnp.where(in_head, zn + rot, 0.0))

  q_scale = MLA_QK ** -0.5 * LOG2E
  for hh in range(MLA_HEADS):
    sl = slice(LANES * hh, LANES * (hh + 1))
    q_ref[:, sl] = (head_norm_rope(zq[:, sl], gq_ref[...]) * q_scale).astype(BF16)
    k_ref[:, sl] = head_norm_rope(zk[:, sl] + kr, gk_ref[...]).astype(BF16)
    v_ref[:, sl] = jnp.where(lane == MLA_V, 1.0, zv[:, sl]).astype(BF16)


def _rot_cols(w):
  half = MLA_ROPE // 2
  return jnp.concatenate([-w[..., half:], w[..., :half]], axis=-1)


def _swap_halves(g):
  half = MLA_ROPE // 2
  return jnp.concatenate([g[..., half:], g[..., :half]], axis=-1)


def _even_in(h, pos, norm, w_in, q_norm, w_uq, kv_norm, w_ukv, q_hnorm, k_hnorm):
  t = h.shape[0]
  tm = min(TOKEN_TILE, t)
  w_cq = w_in[:, :MLA_Q_RANK]
  w_ckv = w_in[:, MLA_Q_RANK:MLA_Q_RANK + MLA_KV_RANK]
  w_kr = w_in[:, MLA_Q_RANK + MLA_KV_RANK:MLA_IN]
  w_kr_ext = jnp.concatenate([jnp.zeros((D_MODEL, MLA_NOPE), F32), w_kr, _rot_cols(w_kr)], axis=1)
  wa = jnp.concatenate([w_cq, w_ckv, w_kr_ext], axis=1).astype(BF16)
  wrw = w_in[:, MLA_IN:].astype(BF16)

  wq3 = w_uq.reshape(MLA_Q_RANK, MLA_HEADS, MLA_QK)
  wq_ext = jnp.concatenate([wq3, _rot_cols(wq3[..., MLA_NOPE:])], axis=-1)
  wq_ext = wq_ext.reshape(MLA_Q_RANK, MLA_HEADS * LANES).astype(BF16)
  wkv3 = w_ukv.reshape(MLA_KV_RANK, MLA_HEADS, MLA_NOPE + MLA_V)
  pad = jnp.zeros((MLA_KV_RANK, MLA_HEADS, LANES - MLA_NOPE), F32)
  wk_ext = jnp.concatenate([wkv3[..., :MLA_NOPE], pad], axis=-1)
  wk_ext = wk_ext.reshape(MLA_KV_RANK, MLA_HEADS * LANES).astype(BF16)
  wv_ext = jnp.concatenate([wkv3[..., MLA_NOPE:], pad], axis=-1)
  wv_ext = wv_ext.reshape(MLA_KV_RANK, MLA_HEADS * LANES).astype(BF16)

  def g_ext(g):
    return jnp.concatenate([g, _swap_halves(g[MLA_NOPE:])]).reshape(1, LANES)

  half = MLA_ROPE // 2
  inv_freq = ROPE_BASE ** (-jnp.arange(half, dtype=F32) / half)
  invf = jnp.concatenate([jnp.zeros((MLA_NOPE,), F32)] + [inv_freq] * 4).reshape(1, LANES)

  args = (h, pos.reshape(t, 1), norm.reshape(1, -1), wa, wrw, q_norm.reshape(1, -1), wq_ext,
          kv_norm.reshape(1, -1), wk_ext, wv_ext, g_ext(q_hnorm), g_ext(k_hnorm), invf)
  in_specs = [_rows(tm, D_MODEL), _rows(tm, 1)] + [_resident(a) for a in args[2:]]
  hw = MLA_HEADS * LANES
  return pl.pallas_call(
      _even_in_kernel,
      grid=(t // tm,),
      in_specs=in_specs,
      out_specs=[_rows(tm, hw), _rows(tm, hw), _rows(tm, hw), _rows(tm, RW_IN)],
      out_shape=[jax.ShapeDtypeStruct((t, hw), BF16)] * 3 + [jax.ShapeDtypeStruct((t, RW_IN), F32)],
      compiler_params=_cparams("parallel"),
      name="even_in",
  )(*args)


def _flash_kernel(q_ref, k_ref, v_ref, o_ref, s_sc, *, tile):
  qi = pl.program_id(2)
  q = q_ref[...]
  row = lax.broadcasted_iota(jnp.int32, (tile, tile), 0)
  col = lax.broadcasted_iota(jnp.int32, (tile, tile), 1)

  def scores(j):
    return _dot_nt(q, k_ref[pl.ds(pl.multiple_of(j * tile, tile), tile), :])

  def accumulate(j, m, acc, s):
    part = s[:, :LANES]
    for c in range(1, tile // LANES):
      part = jnp.maximum(part, s[:, LANES * c:LANES * (c + 1)])
    m_new = jnp.maximum(m, jnp.max(part, axis=-1, keepdims=True))
    p = jnp.exp2(s - m_new).astype(BF16)
    pv = jnp.dot(p, v_ref[pl.ds(pl.multiple_of(j * tile, tile), tile), :],
                 preferred_element_type=F32)
    return m_new, acc * jnp.exp2(m - m_new) + pv

  def diagonal(s):
    return jnp.where(col <= row, s, NEG_BIG)

  def body(i, carry):
    m, acc = carry
    j = 2 * i
    s_sc[1] = scores(j + 1)
    m, acc = accumulate(j, m, acc, s_sc[0])
    s_sc[0] = scores(j + 2)
    return accumulate(j + 1, m, acc, s_sc[1])

  def odd_tail(m, acc):
    s_sc[1] = scores(qi)
    m, acc = accumulate(qi - 1, m, acc, s_sc[0])
    return accumulate(qi, m, acc, diagonal(s_sc[1]))

  def even_tail(m, acc):
    return accumulate(qi, m, acc, diagonal(s_sc[0]))

  s_sc[0] = scores(0)
  carry = (jnp.full((tile, 1), NEG_BIG, F32), jnp.zeros((tile, LANES), F32))
  m, acc = lax.fori_loop(0, qi // 2, body, carry)
  _, acc = lax.cond(qi % 2 == 1, odd_tail, even_tail, m, acc)
  o_ref[...] = (acc / acc[:, MLA_V:MLA_V + 1]).astype(BF16)


def _flash(q, k, v, batch, seq):
  tile = min(FLASH_TILE, seq)
  nq = seq // tile
  return pl.pallas_call(
      functools.partial(_flash_kernel, tile=tile),
      grid=(batch, MLA_HEADS, nq),
      in_specs=[
          pl.BlockSpec((tile, LANES), lambda b, h, i: (b * nq + i, h)),
          pl.BlockSpec((seq, LANES), lambda b, h, i: (b, h)),
          pl.BlockSpec((seq, LANES), lambda b, h, i: (b, h)),
      ],
      out_specs=pl.BlockSpec((tile, LANES), lambda b, h, i: (b * nq + i, h)),
      out_shape=jax.ShapeDtypeStruct(q.shape, BF16),
      scratch_shapes=[pltpu.VMEM((2, tile, tile), F32)],
      compiler_params=_cparams("parallel", "parallel", "arbitrary"),
      name="flash",
  )(q, k, v)


def _unit_lower_inverses(l_abs, row, col):
  eye = (row == col).astype(F32)
  same16 = (row // 16) == (col // 16)
  same32 = (row // 32) == (col // 32)
  off16 = same32 & jnp.logical_not(same16)
  x1 = [jnp.where(same16, l, 0.0).astype(BF16) for l in l_abs]
  x2 = [_dot(x, x).astype(BF16) for x in x1]
  x4 = [_dot(x, x).astype(BF16) for x in x2]
  x8 = [_dot(x, x).astype(BF16) for x in x4]
  ts = [eye + x.astype(F32) for x in x1]
  for xs in (x2, x4, x8):
    ts = [t + _dot(t, x) for t, x in zip(ts, xs)]
  for keep in (off16, jnp.logical_not(same32)):
    los = [jnp.where(keep, l, 0.0).astype(BF16) for l in l_abs]
    tbs = [t.astype(BF16) for t in ts]
    mids = [_dot(lo, tb) for lo, tb in zip(los, tbs)]
    ts = [t + _dot(tb, mid) for t, tb, mid in zip(ts, tbs, mids)]
  return ts


def _rwkv_kernel(p_ref, mu_ref, w0_ref, w2_ref, a0_ref, a2_ref, g2_ref, kk_ref, ka_ref, rk_ref,
                 lng_ref, lnb_ref, seg_ref, cum_ref, all_ref, y_ref, prev_sc, state_sc, y_sc, *, tm, nb):
  rows = nb * tm

  @pl.when(pl.program_id(0) == 0)
  def _():
    prev_sc[...] = jnp.zeros_like(prev_sc)
    state_sc[...] = jnp.zeros_like(state_sc)

  p = p_ref[...].reshape(rows, RW_IN)
  rowid = lax.broadcasted_iota(jnp.int32, (rows, 1), 0)
  prev = pltpu.roll(p, 1, 0)
  for b in range(nb):
    prev = jnp.where(rowid == b * tm, prev_sc[8 * b + 7:8 * b + 8, :], prev)
  for b in range(nb):
    prev_sc[8 * b:8 * (b + 1), :] = p[(b + 1) * tm - 8:(b + 1) * tm, :]
  xs = p + (prev - p) * mu_ref[...]
  r = xs[:, :RW_DIM]
  k = xs[:, RW_DIM:2 * RW_DIM]
  v = xs[:, 2 * RW_DIM:3 * RW_DIM]
  lora_in = xs[:, 3 * RW_DIM:3 * RW_DIM + LANES]
  xg = xs[:, 3 * RW_DIM + LANES:]

  w_log = -_softplus(-(w0_ref[...] + _dot(jnp.tanh(lora_in), w2_ref[...]))) - 0.5
  logw = -jnp.exp(w_log)
  a = _sigmoid(a0_ref[...] + _dot(lora_in, a2_ref[...]))
  gate = _dot(_sigmoid(xg), g2_ref[...])
  seg = seg_ref[...]
  kk = k * kk_ref[...]
  kk = kk / jnp.maximum(jnp.sqrt(_dot_exact_rhs(kk * kk, seg)), 1e-12)
  km = k * (1.0 + (a - 1.0) * ka_ref[...])
  b_s = kk * a

  g_in = _dot_exact_lhs(cum_ref[...], logw)
  g_end = _dot_exact_lhs(all_ref[...], logw)
  e_in = jnp.exp(g_in)
  e_neg = jnp.exp(-g_in)
  e_end = jnp.exp(g_end - g_in)
  at = -kk * jnp.exp(g_in - logw)
  bt = b_s * e_neg
  kt = km * e_neg
  rt = r * e_in
  bh = b_s * e_end
  kh = km * e_end
  e_tot = jnp.exp(g_end)

  lane = lax.broadcasted_iota(jnp.int32, (1, LANES), 1)
  first = lane < RW_HEAD
  row = lax.broadcasted_iota(jnp.int32, (LANES, LANES), 0)
  col = lax.broadcasted_iota(jnp.int32, (LANES, LANES), 1)
  same_head = (row // RW_CHUNK) == (col // RW_CHUNK)
  strict = same_head & (col < row)
  incl = same_head & (col <= row)

  def stack(x):
    return jnp.concatenate([jnp.where(first, x, 0.0), jnp.where(first, 0.0, x)], axis=0)

  def dup(x):
    return jnp.concatenate([x, x], axis=0)

  n_pairs = RW_HEADS // 2
  units = [(c, b, pr) for c in range(tm // RW_CHUNK) for b in range(nb) for pr in range(n_pairs)]

  def tile_of(x, c, b, pr):
    r0 = b * tm + RW_CHUNK * c
    return x[r0:r0 + RW_CHUNK, LANES * pr:LANES * (pr + 1)]

  sa = [stack(tile_of(at, *u)).astype(BF16) for u in units]
  sr = [stack(tile_of(rt, *u)).astype(BF16) for u in units]
  sv = [stack(tile_of(v, *u)).astype(BF16) for u in units]
  sc = [_dot_nt(jnp.concatenate([a_, r_], axis=0),
                jnp.concatenate([dup(tile_of(bt, *u)), dup(tile_of(kt, *u))], axis=0))
        for a_, r_, u in zip(sa, sr, units)]
  l_ab = [jnp.where(strict, x[:LANES, :LANES], 0.0) for x in sc]
  l_ak = [jnp.where(strict, x[:LANES, LANES:], 0.0).astype(BF16) for x in sc]
  l_rb = [jnp.where(incl, x[LANES:, :LANES], 0.0).astype(BF16) for x in sc]
  l_rk = [jnp.where(incl, x[LANES:, LANES:], 0.0).astype(BF16) for x in sc]
  bk_t = [jnp.concatenate([stack(tile_of(bh, *u)), stack(tile_of(kh, *u))], axis=0).T.astype(BF16)
          for u in units]
  dec = [jnp.broadcast_to(tile_of(e_tot, *u)[0:1, :], (LANES, LANES)).T for u in units]
  tinv = [t.astype(BF16) for t in _unit_lower_inverses(l_ab, row, col)]
  w1r = [jnp.concatenate([_dot(t, a_).astype(BF16), r_], axis=0) for t, a_, r_ in zip(tinv, sa, sr)]
  lakv = [_dot(l, v_) for l, v_ in zip(l_ak, sv)]
  w2 = [_dot(t, x) for t, x in zip(tinv, lakv)]
  yv = [_dot(l, v_) for l, v_ in zip(l_rk, sv)]

  n_states = nb * n_pairs
  states = [state_sc[i] for i in range(n_states)]
  for c in range(tm // RW_CHUNK):
    ids = range(c * n_states, (c + 1) * n_states)
    both = [_dot(w1r[i], states[i % n_states]) for i in ids]
    us = [bo[:LANES] + w2[i] for bo, i in zip(both, ids)]
    ubs = [u.astype(BF16) for u in us]
    ys = [bo[LANES:] + _dot(l_rb[i], ub) + yv[i] for bo, ub, i in zip(both, ubs, ids)]
    upds = [_dot(bk_t[i], jnp.concatenate([ub, sv[i]], axis=0)) for ub, i in zip(ubs, ids)]
    for y, upd, i in zip(ys, upds, ids):
      _, b, pr = units[i]
      r0 = b * tm + RW_CHUNK * c
      y_sc[r0:r0 + RW_CHUNK, LANES * pr:LANES * (pr + 1)] = y[:RW_CHUNK] + y[RW_CHUNK:]
      states[i % n_states] = states[i % n_states] * dec[i] + upd
  for i in range(n_states):
    state_sc[i] = states[i]

  y = y_sc[...]
  inv_n = 1.0 / RW_HEAD
  mean = _dot_exact_rhs(y, seg) * inv_n
  d = y - mean
  var = _dot_exact_rhs(d * d, seg) * inv_n
  yn = d * lax.rsqrt(var + RW_LN_EPS) * lng_ref[...] + lnb_ref[...]
  bonus = _dot_exact_rhs(r * km * rk_ref[...], seg) * v
  y_ref[...] = ((yn + bonus) * gate).astype(BF16).reshape(nb, tm, RW_DIM)


def _rwkv(prw, batch, seq, mu, w0, w2, a0, a2, g2, k_k, k_a, r_k, ln_g, ln_b):
  tm = min(RW_TILE, seq)
  rows = batch * tm
  zeros = jnp.zeros((RW_DECAY_LORA, RW_DIM), F32)
  w2e = jnp.concatenate([w2, zeros], axis=0).astype(BF16)
  a2e = jnp.concatenate([zeros, a2], axis=0).astype(BF16)
  idx = jnp.arange(RW_DIM) // RW_HEAD
  seg = (idx[:, None] == idx[None, :]).astype(BF16)
  tok = jnp.arange(rows)
  same_chunk = (tok[:, None] // RW_CHUNK) == (tok[None, :] // RW_CHUNK)
  cum = (same_chunk & (tok[None, :] <= tok[:, None])).astype(BF16)
  allc = same_chunk.astype(BF16)
  row = lambda x: x.reshape(1, -1)
  args = (prw.reshape(batch, seq, RW_IN), row(mu), row(w0), w2e, row(a0), a2e, g2.astype(BF16),
          row(k_k), row(k_a), row(r_k), row(ln_g), row(ln_b), seg, cum, allc)
  in_specs = [pl.BlockSpec((batch, tm, RW_IN), lambda i: (0, i, 0))]
  in_specs += [_resident(a) for a in args[1:]]
  out = pl.pallas_call(
      functools.partial(_rwkv_kernel, tm=tm, nb=batch),
      grid=(seq // tm,),
      in_specs=in_specs,
      out_specs=pl.BlockSpec((batch, tm, RW_DIM), lambda i: (0, i, 0)),
      out_shape=jax.ShapeDtypeStruct((batch, seq, RW_DIM), BF16),
      scratch_shapes=[pltpu.VMEM((8 * batch, RW_IN), F32),
                      pltpu.VMEM((batch * (RW_HEADS // 2), LANES, LANES), F32),
                      pltpu.VMEM((rows, RW_DIM), F32)],
      compiler_params=_cparams("arbitrary"),
      name="rwkv",
  )(*args)
  return out.reshape(batch * seq, RW_DIM)


def _proj_res_kernel(*refs, n):
  res_ref, o_ref = refs[0], refs[-1]
  acc = res_ref[...]
  for i in range(n):
    acc = acc + jnp.dot(refs[1 + i][...], refs[1 + n + i][...], preferred_element_type=F32)
  o_ref[...] = acc


def _proj_res(res, xs, ws):
  t = res.shape[0]
  tm = min(TOKEN_TILE, t)
  n = len(xs)
  in_specs = [_rows(tm, D_MODEL)] + [_rows(tm, x.shape[1]) for x in xs] + [_resident(w) for w in ws]
  return pl.pallas_call(
      functools.partial(_proj_res_kernel, n=n),
      grid=(t // tm,),
      in_specs=in_specs,
      out_specs=_rows(tm, D_MODEL),
      out_shape=jax.ShapeDtypeStruct((t, D_MODEL), F32),
      compiler_params=_cparams("parallel"),
      name="proj_res",
  )(res, *xs, *ws)


def _memkv_kernel(mem_ref, g_ref, w_ref, kg_ref, k_ref, v_ref):
  xn = _rms(mem_ref[...], g_ref[...]).astype(BF16)
  kv = jnp.dot(xn, w_ref[...], preferred_element_type=F32)
  for hh in range(X_HEADS):
    sl = slice(X_HEAD * hh, X_HEAD * (hh + 1))
    k_ref[:, sl] = _rms(kv[:, sl], kg_ref[...]).astype(BF16)
  v_ref[...] = kv[:, X_DIM:].astype(BF16)


def _memkv(mem2, batch, g, wkv, k_hnorm):
  args = (mem2, g.reshape(1, -1), wkv.astype(BF16), k_hnorm.reshape(1, -1))
  return pl.pallas_call(
      _memkv_kernel,
      grid=(batch,),
      in_specs=[_rows(MEM_LEN, D_MODEL)] + [_resident(a) for a in args[1:]],
      out_specs=[_rows(MEM_LEN, X_DIM)] * 2,
      out_shape=[jax.ShapeDtypeStruct((batch * MEM_LEN, X_DIM), BF16)] * 2,
      compiler_params=_cparams("parallel"),
      name="memkv",
  )(*args)


def _xattn_kernel(h_ref, k_ref, v_ref, g_ref, wq_ref, qg_ref, wo_ref, o_ref):
  h = h_ref[...]
  xn = _rms(h, g_ref[...]).astype(BF16)
  q = jnp.dot(xn, wq_ref[...], preferred_element_type=F32)
  scale = X_HEAD ** -0.5
  outs = []
  for hh in range(X_HEADS):
    sl = slice(X_HEAD * hh, X_HEAD * (hh + 1))
    qh = _rms(q[:, sl], qg_ref[...]) * scale
    s = _dot_nt(qh, k_ref[:, sl])
    e = jnp.exp(s - jnp.max(s, axis=-1, keepdims=True))
    p = e / jnp.sum(e, axis=-1, keepdims=True)
    outs.append(jnp.dot(p.astype(BF16), v_ref[:, sl], preferred_element_type=F32))
  att = jnp.concatenate(outs, axis=-1).astype(BF16)
  o_ref[...] = h + jnp.dot(att, wo_ref[...], preferred_element_type=F32)


def _xattn(h, kmem, vmem, seq, g, wq, q_hnorm, wo):
  t = h.shape[0]
  tm = min(TOKEN_TILE, seq)
  per_batch = seq // tm
  args = (h, kmem, vmem, g.reshape(1, -1), wq.astype(BF16), q_hnorm.reshape(1, -1), wo.astype(BF16))
  mem_spec = pl.BlockSpec((MEM_LEN, X_DIM), lambda i: (i // per_batch, 0))
  return pl.pallas_call(
      _xattn_kernel,
      grid=(t // tm,),
      in_specs=[_rows(tm, D_MODEL), mem_spec, mem_spec] + [_resident(a) for a in args[3:]],
      out_specs=_rows(tm, D_MODEL),
      out_shape=jax.ShapeDtypeStruct((t, D_MODEL), F32),
      compiler_params=_cparams("parallel"),
      name="xattn",
  )(*args)


FFN_CHUNK = 256


def _ffn_kernel(h_ref, g_ref, w1_ref, w3_ref, w2_ref, o_ref):
  h = h_ref[...]
  xn = _rms(h, g_ref[...]).astype(BF16)
  acc = h
  for c in range(FFN_HIDDEN // FFN_CHUNK):
    sl = slice(FFN_CHUNK * c, FFN_CHUNK * (c + 1))
    u1 = jnp.dot(xn, w1_ref[:, sl], preferred_element_type=F32)
    u3 = jnp.dot(xn, w3_ref[:, sl], preferred_element_type=F32)
    act = (_silu(u1) * u3).astype(BF16)
    acc = acc + jnp.dot(act, w2_ref[sl, :], preferred_element_type=F32)
  o_ref[...] = acc


def _ffn(h, g, w13, w2):
  t = h.shape[0]
  tm = min(TOKEN_TILE, t)
  args = (h, g.reshape(1, -1), w13[:, :FFN_HIDDEN].astype(BF16), w13[:, FFN_HIDDEN:].astype(BF16),
          w2.astype(BF16))
  return pl.pallas_call(
      _ffn_kernel,
      grid=(t // tm,),
      in_specs=[_rows(tm, D_MODEL)] + [_resident(a) for a in args[1:]],
      out_specs=_rows(tm, D_MODEL),
      out_shape=jax.ShapeDtypeStruct((t, D_MODEL), F32),
      compiler_params=_cparams("parallel"),
      name="ffn",
  )(*args)


def _odd_in_kernel(h_ref, g_ref, wz_ref, wx_ref, wdt_ref, cw_ref, cb_ref, z_ref, xbc_ref, dt_ref,
                   xbuf_sc, *, tm, tiles_per_seq):
  @pl.when(pl.program_id(0) % tiles_per_seq == 0)
  def _():
    xbuf_sc[0:8, :] = jnp.zeros((8, SSM_CONV_DIM), F32)

  xn = _rms(h_ref[...], g_ref[...]).astype(BF16)
  dt_ref[...] = jnp.dot(xn, wdt_ref[...], preferred_element_type=F32)
  n_blocks = SSM_CONV_DIM // CONV_COLS

  def project(c):
    cs = slice(CONV_COLS * c, CONV_COLS * (c + 1))
    xbuf_sc[8:8 + tm, cs] = jnp.dot(xn, wx_ref[:, cs], preferred_element_type=F32)

  project(0)
  for c in range(n_blocks):
    cs = slice(CONV_COLS * c, CONV_COLS * (c + 1))
    if c + 1 < n_blocks:
      project(c + 1)
    xe = xbuf_sc[:, cs]
    acc = cw_ref[0:1, cs] * xe
    for j in range(1, SSM_CONV):
      acc = cw_ref[j:j + 1, cs] * xe + pltpu.roll(acc, 1, 0)
    xbuf_sc[0:8, cs] = xe[tm:tm + 8, :]
    xbc_ref[:, cs] = _silu(acc[8:, :] + cb_ref[:, cs]).astype(BF16)
    zs = slice(CONV_COLS * c, min(CONV_COLS * (c + 1), SSM_INNER))
    if zs.start < SSM_INNER:
      z_ref[:, zs] = jnp.dot(xn, wz_ref[:, zs], preferred_element_type=F32).astype(BF16)


def _odd_in(h, seq, g, w_in, conv_w, conv_b):
  t = h.shape[0]
  tm = min(TOKEN_TILE, seq)
  wz = w_in[:, :SSM_INNER].astype(BF16)
  wx = w_in[:, SSM_INNER:SSM_INNER + SSM_CONV_DIM].astype(BF16)
  wdt = w_in[:, SSM_INNER + SSM_CONV_DIM:]
  wdt = jnp.concatenate([wdt, jnp.zeros((D_MODEL, LANES - SSM_HEADS), F32)], axis=1).astype(BF16)
  args = (h, g.reshape(1, -1), wz, wx, wdt, conv_w, conv_b.reshape(1, -1))
  return pl.pallas_call(
      functools.partial(_odd_in_kernel, tm=tm, tiles_per_seq=seq // tm),
      grid=(t // tm,),
      in_specs=[_rows(tm, D_MODEL)] + [_resident(a) for a in args[1:]],
      out_specs=[_rows(tm, SSM_INNER), _rows(tm, SSM_CONV_DIM), _rows(tm, LANES)],
      out_shape=[jax.ShapeDtypeStruct((t, SSM_INNER), BF16),
                 jax.ShapeDtypeStruct((t, SSM_CONV_DIM), BF16),
                 jax.ShapeDtypeStruct((t, LANES), F32)],
      scratch_shapes=[pltpu.VMEM((tm + 8, SSM_CONV_DIM), F32)],
      compiler_params=_cparams("arbitrary"),
      name="odd_in",
  )(*args)


def _ssd_kernel(xbc_ref, z_ref, dt_ref, dtb_ref, alog_ref, dskip_ref, gn_ref, exp_ref, tri_ref,
                y_ref, state_sc, yacc_sc, *, L):
  @pl.when(pl.program_id(1) == 0)
  def _():
    state_sc[...] = jnp.zeros_like(state_sc)

  gn_w = SSM_GROUPS * SSM_STATE
  x = xbc_ref[:, :SSM_INNER]
  b_all = xbc_ref[:, SSM_INNER:SSM_INNER + gn_w]
  c_all = xbc_ref[:, SSM_INNER + gn_w:]

  dt = _softplus(dt_ref[...] + dtb_ref[...])
  a2 = dt * (-jnp.exp(alog_ref[...]) * LOG2E)
  cum = _dot_exact_lhs(tri_ref[...], a2)
  last = cum[L - 1:L, :]
  key_t = (cum - jnp.log2(dt)).T
  expand = exp_ref[...]
  ecum_x = _dot(jnp.exp2(cum), expand)
  wend_x = _dot(dt * jnp.exp2(last - cum), expand)
  elast_x = _dot_exact_rhs(jnp.broadcast_to(jnp.exp2(last), (8, LANES)), expand)[0:1, :]
  xend = (x * wend_x).astype(BF16)

  row = lax.broadcasted_iota(jnp.int32, (L, L), 0)
  col = lax.broadcasted_iota(jnp.int32, (L, L), 1)
  causal = col <= row
  lane = lax.broadcasted_iota(jnp.int32, (1, LANES), 1)
  first = lane < SSM_HEAD
  zero = jnp.zeros((), BF16)

  e_per = SSM_HEADS // SSM_GROUPS
  for g in range(SSM_GROUPS):
    gs = slice(SSM_GROUP_W * g, SSM_GROUP_W * (g + 1))
    bg = b_all[:, SSM_STATE * g:SSM_STATE * (g + 1)]
    cg = c_all[:, SSM_STATE * g:SSM_STATE * (g + 1)]
    cb = _dot_nt(cg, bg)
    st = state_sc[:, gs]
    yacc_sc[:, gs] = _dot(cg, st) * ecum_x[:, gs]
    for pr in range(e_per // 2):
      ps = slice(SSM_GROUP_W * g + LANES * pr, SSM_GROUP_W * g + LANES * (pr + 1))
      xp = x[:, ps]
      y_pair = jnp.zeros((L, LANES), F32)
      for j in range(2):
        hd = e_per * g + 2 * pr + j
        decay = jnp.exp2(cum[:, hd:hd + 1] - key_t[hd:hd + 1, :])
        m = jnp.where(causal, cb * decay, 0.0).astype(BF16)
        xj = jnp.where(first, xp, zero) if j == 0 else jnp.where(first, zero, xp)
        y_pair = y_pair + jnp.dot(m, xj, preferred_element_type=F32)
      yacc_sc[:, ps] += y_pair
    state_sc[:, gs] = st * elast_x[:, gs] + _dot_tn(bg, xend[:, gs])

  y = yacc_sc[...] + x * dskip_ref[...]
  y = y * _silu(z_ref[...].astype(F32))
  for g in range(SSM_GROUPS):
    gs = slice(SSM_GROUP_W * g, SSM_GROUP_W * (g + 1))
    y_ref[:, gs] = _rms(y[:, gs], gn_ref[:, gs]).astype(BF16)


def _ssd(z, xbc, dt, batch, seq, dt_bias, a_log, d_skip, gnorm):
  L = min(SSM_CHUNK, seq)
  nc = seq // L
  pad = lambda x: jnp.concatenate([x, jnp.zeros((LANES - SSM_HEADS,), F32)]).reshape(1, LANES)
  head_of_lane = jnp.arange(SSM_INNER) // SSM_HEAD
  expand = (jnp.arange(LANES)[:, None] == head_of_lane[None, :]).astype(BF16)
  tok = jnp.arange(L)
  tri = (tok[None, :] <= tok[:, None]).astype(BF16)
  args = (xbc, z, dt, pad(dt_bias), pad(a_log), jnp.repeat(d_skip, SSM_HEAD).reshape(1, -1),
          gnorm.reshape(1, -1), expand, tri)
  tiles = lambda w: pl.BlockSpec((L, w), lambda b, i: (b * nc + i, 0))
  return pl.pallas_call(
      functools.partial(_ssd_kernel, L=L),
      grid=(batch, nc),
      in_specs=[tiles(SSM_CONV_DIM), tiles(SSM_INNER), tiles(LANES)] + [_resident(a) for a in args[3:]],
      out_specs=tiles(SSM_INNER),
      out_shape=jax.ShapeDtypeStruct((batch * seq, SSM_INNER), BF16),
      scratch_shapes=[pltpu.VMEM((SSM_STATE, SSM_INNER), F32),
                      pltpu.VMEM((L, SSM_INNER), F32)],
      compiler_params=_cparams("parallel", "arbitrary"),
      name="ssd",
  )(*args)


def kernel(x, mem, positions, ev_norm, ev_w_in, mla_q_norm, mla_w_uq, mla_kv_norm, mla_w_ukv, mla_q_hnorm, mla_k_hnorm, rw_mu, rw_w0, rw_w2, rw_a0, rw_a2, rw_g2, rw_k_k, rw_k_a, rw_r_k, rw_ln_g, rw_ln_b, ev_w_out, od_norm, od_w_in, ssm_conv_w, ssm_conv_b, ssm_dt_bias, ssm_a_log, ssm_d, ssm_gnorm, od_w_out, xa_norm_x, xa_norm_mem, xa_wq, xa_wkv, xa_q_hnorm, xa_k_hnorm, xa_wo, ffn_norm, ffn_w13, ffn_w2):
  batch, seq, _ = x.shape
  depth = xa_wq.shape[0]
  h = x.reshape(batch * seq, D_MODEL)
  mem2 = mem.reshape(batch * MEM_LEN, D_MODEL)
  pos = positions.reshape(batch * seq)
  for i in range(depth):
    j = i // 2
    if i % 2 == 0:
      q, k, v, prw = _even_in(h, pos, ev_norm[j], ev_w_in[j], mla_q_norm[j], mla_w_uq[j],
                              mla_kv_norm[j], mla_w_ukv[j], mla_q_hnorm[j], mla_k_hnorm[j])
      y_mla = _flash(q, k, v, batch, seq)
      y_rw = _rwkv(prw, batch, seq, rw_mu[j], rw_w0[j], rw_w2[j], rw_a0[j], rw_a2[j], rw_g2[j],
                   rw_k_k[j], rw_k_a[j], rw_r_k[j].reshape(-1), rw_ln_g[j], rw_ln_b[j])
      w_out = ev_w_out[j]
      w_mla = w_out[:MLA_HEADS * MLA_V].reshape(MLA_HEADS, MLA_V, D_MODEL)
      w_mla = jnp.concatenate([w_mla, jnp.zeros((MLA_HEADS, LANES - MLA_V, D_MODEL), F32)], axis=1)
      w_mla = w_mla.reshape(MLA_HEADS * LANES, D_MODEL).astype(BF16)
      h = _proj_res(h, [y_mla, y_rw], [w_mla, w_out[MLA_HEADS * MLA_V:].astype(BF16)])
    else:
      z, xbc, dt = _odd_in(h, seq, od_norm[j], od_w_in[j], ssm_conv_w[j], ssm_conv_b[j])
      y = _ssd(z, xbc, dt, batch, seq, ssm_dt_bias[j], ssm_a_log[j], ssm_d[j], ssm_gnorm[j])
      h = _proj_res(h, [y], [od_w_out[j].astype(BF16)])
    kmem, vmem = _memkv(mem2, batch, xa_norm_mem[i], xa_wkv[i], xa_k_hnorm[i])
    h = _xattn(h, kmem, vmem, seq, xa_norm_x[i], xa_wq[i], xa_q_hnorm[i], xa_wo[i])
    h = _ffn(h, ffn_norm[i], ffn_w13[i], ffn_w2[i])
  return h.reshape(batch, seq, D_MODEL)
```

```python
import functools
import math

import jax
import jax.numpy as jnp
from jax import lax
from jax.experimental import pallas as pl
from jax.experimental.pallas import tpu as pltpu

F32 = jnp.float32
BF16 = jnp.bfloat16

D_MODEL = 1024
MEM_LEN = 256
NORM_EPS = 1e-6

MLA_HEADS = 8
MLA_NOPE = 64
MLA_ROPE = 32
MLA_QK = MLA_NOPE + MLA_ROPE
MLA_V = 64
MLA_Q_RANK = 384
MLA_KV_RANK = 256
ROPE_BASE = 10000.0
MLA_IN = MLA_Q_RANK + MLA_KV_RANK + MLA_ROPE

RW_HEADS = 8
RW_HEAD = 64
RW_DIM = RW_HEADS * RW_HEAD
RW_DECAY_LORA = 64
RW_AAA_LORA = 64
RW_GATE_LORA = 128
RW_LN_EPS = 64e-5
RW_IN = 3 * RW_DIM + RW_DECAY_LORA + RW_AAA_LORA + RW_GATE_LORA
RW_CHUNK = 64

SSM_INNER = 2 * D_MODEL
SSM_HEAD = 64
SSM_HEADS = SSM_INNER // SSM_HEAD
SSM_GROUPS = 4
SSM_STATE = 128
SSM_CONV = 4
SSM_CHUNK = 256
SSM_CONV_DIM = SSM_INNER + 2 * SSM_GROUPS * SSM_STATE
SSM_GROUP_W = SSM_INNER // SSM_GROUPS

X_HEADS = 4
X_HEAD = 128
X_DIM = X_HEADS * X_HEAD

FFN_HIDDEN = -((-8 * D_MODEL) // (3 * 256)) * 256

LANES = 128
V7X_VMEM_BYTES = 64 * 1024 * 1024
VMEM_LIMIT = V7X_VMEM_BYTES - 8 * 1024 * 1024

TOKEN_TILE = 512
RW_TILE = 128
FLASH_TILE = 1024
CONV_COLS = 512
NEG_BIG = -1e30
LOG2E = 1.4426950408889634


def _cparams(*sem):
  return pltpu.CompilerParams(dimension_semantics=sem, vmem_limit_bytes=VMEM_LIMIT)


def _resident(arr):
  nd = arr.ndim
  return pl.BlockSpec(arr.shape, lambda *_: (0,) * nd, pipeline_mode=pl.Buffered(1))


def _rows(tm, width):
  return pl.BlockSpec((tm, width), lambda i: (i, 0))


def _rms(x, g, eps=NORM_EPS):
  return x * lax.rsqrt(jnp.mean(x * x, axis=-1, keepdims=True) + eps) * g


def _dot(a, b):
  return jnp.dot(a.astype(BF16), b.astype(BF16), preferred_element_type=F32)


def _dot_nt(a, b):
  return lax.dot_general(a.astype(BF16), b.astype(BF16), (((1,), (1,)), ((), ())),
                         preferred_element_type=F32)


def _dot_tn(a, b):
  return lax.dot_general(a.astype(BF16), b.astype(BF16), (((0,), (0,)), ((), ())),
                         preferred_element_type=F32)


def _split(x):
  hi = x.astype(BF16)
  lo = (x - hi.astype(F32)).astype(BF16)
  return hi, lo


def _dot_exact_lhs(m, x):
  hi, lo = _split(x)
  return (jnp.dot(m, hi, preferred_element_type=F32) +
          jnp.dot(m, lo, preferred_element_type=F32))


def _dot_exact_rhs(x, m):
  hi, lo = _split(x)
  return (jnp.dot(hi, m, preferred_element_type=F32) +
          jnp.dot(lo, m, preferred_element_type=F32))


def _sigmoid(x):
  return 1.0 / (1.0 + jnp.exp(-x))


def _silu(x):
  return x * _sigmoid(x)


def _softplus(x):
  return jnp.maximum(x, 0.0) + jnp.log(1.0 + jnp.exp(-jnp.abs(x)))


def _even_in_kernel(h_ref, pos_ref, g_ref, wa_ref, wrw_ref, qn_ref, wq_ref, kvn_ref, wk_ref,
                    wv_ref, gq_ref, gk_ref, invf_ref, q_ref, k_ref, v_ref, prw_ref):
  xn = _rms(h_ref[...], g_ref[...]).astype(BF16)
  pa = jnp.dot(xn, wa_ref[...], preferred_element_type=F32)
  cq = _rms(pa[:, :MLA_Q_RANK], qn_ref[...]).astype(BF16)
  ckv = _rms(pa[:, MLA_Q_RANK:MLA_Q_RANK + MLA_KV_RANK], kvn_ref[...]).astype(BF16)
  kr = pa[:, MLA_Q_RANK + MLA_KV_RANK:]

  lane = lax.broadcasted_iota(jnp.int32, (1, LANES), 1)
  ang = pos_ref[...].astype(F32) * invf_ref[...]
  cs = jnp.where(lane < MLA_NOPE, 1.0, jnp.where(lane < MLA_QK, jnp.cos(ang), jnp.sin(ang)))
  in_head = lane < MLA_QK

  zq = jnp.dot(cq, wq_ref[...], preferred_element_type=F32)
  zk = jnp.dot(ckv, wk_ref[...], preferred_element_type=F32)
  zv = jnp.dot(ckv, wv_ref[...], preferred_element_type=F32)

  def head_norm_rope(z, g_ext):
    ms = jnp.sum(jnp.where(in_head, z * z, 0.0), axis=-1, keepdims=True) * (1.0 / MLA_QK)
    zn = z * lax.rsqrt(ms + NORM_EPS) * g_ext * cs
    rot = pltpu.roll(zn, LANES - MLA_ROPE, 1)
    return jnp.where(lane < MLA_NOPE, zn, jnp.where(in_head, zn + rot, 0.0))

  q_scale = MLA_QK ** -0.5 * LOG2E
  rw_cols = 2 * LANES
  for hh in range(MLA_HEADS):
    sl = slice(LANES * hh, LANES * (hh + 1))
    q_ref[:, sl] = (head_norm_rope(zq[:, sl], gq_ref[...]) * q_scale).astype(BF16)
    k_ref[:, sl] = head_norm_rope(zk[:, sl] + kr, gk_ref[...]).astype(BF16)
    v_ref[:, sl] = jnp.where(lane == MLA_V, 1.0, zv[:, sl]).astype(BF16)
    if hh < RW_IN // rw_cols:
      rws = slice(rw_cols * hh, rw_cols * (hh + 1))
      prw_ref[:, rws] = jnp.dot(xn, wrw_ref[:, rws], preferred_element_type=F32)


def _rot_cols(w):
  half = MLA_ROPE // 2
  return jnp.concatenate([-w[..., half:], w[..., :half]], axis=-1)


def _swap_halves(g):
  half = MLA_ROPE // 2
  return jnp.concatenate([g[..., half:], g[..., :half]], axis=-1)


def _even_in(h, pos, norm, w_in, q_norm, w_uq, kv_norm, w_ukv, q_hnorm, k_hnorm):
  t = h.shape[0]
  tm = min(TOKEN_TILE, t)
  w_cq = w_in[:, :MLA_Q_RANK]
  w_ckv = w_in[:, MLA_Q_RANK:MLA_Q_RANK + MLA_KV_RANK]
  w_kr = w_in[:, MLA_Q_RANK + MLA_KV_RANK:MLA_IN]
  w_kr_ext = jnp.concatenate([jnp.zeros((D_MODEL, MLA_NOPE), F32), w_kr, _rot_cols(w_kr)], axis=1)
  wa = jnp.concatenate([w_cq, w_ckv, w_kr_ext], axis=1).astype(BF16)
  wrw = w_in[:, MLA_IN:].astype(BF16)

  wq3 = w_uq.reshape(MLA_Q_RANK, MLA_HEADS, MLA_QK)
  wq_ext = jnp.concatenate([wq3, _rot_cols(wq3[..., MLA_NOPE:])], axis=-1)
  wq_ext = wq_ext.reshape(MLA_Q_RANK, MLA_HEADS * LANES).astype(BF16)
  wkv3 = w_ukv.reshape(MLA_KV_RANK, MLA_HEADS, MLA_NOPE + MLA_V)
  pad = jnp.zeros((MLA_KV_RANK, MLA_HEADS, LANES - MLA_NOPE), F32)
  wk_ext = jnp.concatenate([wkv3[..., :MLA_NOPE], pad], axis=-1)
  wk_ext = wk_ext.reshape(MLA_KV_RANK, MLA_HEADS * LANES).astype(BF16)
  wv_ext = jnp.concatenate([wkv3[..., MLA_NOPE:], pad], axis=-1)
  wv_ext = wv_ext.reshape(MLA_KV_RANK, MLA_HEADS * LANES).astype(BF16)

  def g_ext(g):
    return jnp.concatenate([g, _swap_halves(g[MLA_NOPE:])]).reshape(1, LANES)

  half = MLA_ROPE // 2
  inv_freq = ROPE_BASE ** (-jnp.arange(half, dtype=F32) / half)
  invf = jnp.concatenate([jnp.zeros((MLA_NOPE,), F32)] + [inv_freq] * 4).reshape(1, LANES)

  args = (h, pos.reshape(t, 1), norm.reshape(1, -1), wa, wrw, q_norm.reshape(1, -1), wq_ext,
          kv_norm.reshape(1, -1), wk_ext, wv_ext, g_ext(q_hnorm), g_ext(k_hnorm), invf)
  in_specs = [_rows(tm, D_MODEL), _rows(tm, 1)] + [_resident(a) for a in args[2:]]
  hw = MLA_HEADS * LANES
  return pl.pallas_call(
      _even_in_kernel,
      grid=(t // tm,),
      in_specs=in_specs,
      out_specs=[_rows(tm, hw), _rows(tm, hw), _rows(tm, hw), _rows(tm, RW_IN)],
      out_shape=[jax.ShapeDtypeStruct((t, hw), BF16)] * 3 + [jax.ShapeDtypeStruct((t, RW_IN), F32)],
      compiler_params=_cparams("parallel"),
      name="even_in",
  )(*args)


def _flash_kernel(q_ref, k_ref, v_ref, o_ref, s_sc, *, tile):
  qi = pl.program_id(2)
  q = q_ref[...]
  row = lax.broadcasted_iota(jnp.int32, (tile, tile), 0)
  col = lax.broadcasted_iota(jnp.int32, (tile, tile), 1)

  def scores(j):
    return _dot_nt(q, k_ref[pl.ds(pl.multiple_of(j * tile, tile), tile), :])

  def accumulate(j, m, acc, s):
    part = s[:, :LANES]
    for c in range(1, tile // LANES):
      part = jnp.maximum(part, s[:, LANES * c:LANES * (c + 1)])
    m_new = jnp.maximum(m, jnp.max(part, axis=-1, keepdims=True))
    p = jnp.exp2(s - m_new).astype(BF16)
    pv = jnp.dot(p, v_ref[pl.ds(pl.multiple_of(j * tile, tile), tile), :],
                 preferred_element_type=F32)
    return m_new, acc * jnp.exp2(m - m_new) + pv

  def diagonal(s):
    return jnp.where(col <= row, s, NEG_BIG)

  def body(i, carry):
    m, acc = carry
    j = 2 * i
    s_sc[1] = scores(j + 1)
    m, acc = accumulate(j, m, acc, s_sc[0])
    s_sc[0] = scores(j + 2)
    return accumulate(j + 1, m, acc, s_sc[1])

  def odd_tail(m, acc):
    s_sc[1] = scores(qi)
    m, acc = accumulate(qi - 1, m, acc, s_sc[0])
    return accumulate(qi, m, acc, diagonal(s_sc[1]))

  def even_tail(m, acc):
    return accumulate(qi, m, acc, diagonal(s_sc[0]))

  s_sc[0] = scores(0)
  carry = (jnp.full((tile, 1), NEG_BIG, F32), jnp.zeros((tile, LANES), F32))
  m, acc = lax.fori_loop(0, qi // 2, body, carry)
  _, acc = lax.cond(qi % 2 == 1, odd_tail, even_tail, m, acc)
  o_ref[...] = (acc / acc[:, MLA_V:MLA_V + 1]).astype(BF16)


def _flash(q, k, v, batch, seq):
  tile = min(FLASH_TILE, seq)
  nq = seq // tile
  return pl.pallas_call(
      functools.partial(_flash_kernel, tile=tile),
      grid=(batch, MLA_HEADS, nq),
      in_specs=[
          pl.BlockSpec((tile, LANES), lambda b, h, i: (b * nq + i, h)),
          pl.BlockSpec((seq, LANES), lambda b, h, i: (b, h)),
          pl.BlockSpec((seq, LANES), lambda b, h, i: (b, h)),
      ],
      out_specs=pl.BlockSpec((tile, LANES), lambda b, h, i: (b * nq + i, h)),
      out_shape=jax.ShapeDtypeStruct(q.shape, BF16),
      scratch_shapes=[pltpu.VMEM((2, tile, tile), F32)],
      compiler_params=_cparams("parallel", "parallel", "arbitrary"),
      name="flash",
  )(q, k, v)


def _unit_lower_inverses(l_abs, row, col):
  eye = (row == col).astype(F32)
  same16 = (row // 16) == (col // 16)
  same32 = (row // 32) == (col // 32)
  off16 = same32 & jnp.logical_not(same16)
  x1 = [jnp.where(same16, l, 0.0).astype(BF16) for l in l_abs]
  x2 = [_dot(x, x).astype(BF16) for x in x1]
  x4 = [_dot(x, x).astype(BF16) for x in x2]
  x8 = [_dot(x, x).astype(BF16) for x in x4]
  ts = [eye + x.astype(F32) for x in x1]
  for xs in (x2, x4, x8):
    ts = [t + _dot(t, x) for t, x in zip(ts, xs)]
  for keep in (off16, jnp.logical_not(same32)):
    los = [jnp.where(keep, l, 0.0).astype(BF16) for l in l_abs]
    tbs = [t.astype(BF16) for t in ts]
    mids = [_dot(lo, tb) for lo, tb in zip(los, tbs)]
    ts = [t + _dot(tb, mid) for t, tb, mid in zip(ts, tbs, mids)]
  return ts


def _rwkv_kernel(p_ref, mu_ref, w0_ref, w2_ref, a0_ref, a2_ref, g2_ref, kk_ref, ka_ref, rk_ref,
                 lng_ref, lnb_ref, seg_ref, cum_ref, all_ref, y_ref, prev_sc, state_sc, y_sc, *, tm, nb):
  rows = nb * tm

  @pl.when(pl.program_id(0) == 0)
  def _():
    prev_sc[...] = jnp.zeros_like(prev_sc)
    state_sc[...] = jnp.zeros_like(state_sc)

  p = p_ref[...].reshape(rows, RW_IN)
  rowid = lax.broadcasted_iota(jnp.int32, (rows, 1), 0)
  prev = pltpu.roll(p, 1, 0)
  for b in range(nb):
    prev = jnp.where(rowid == b * tm, prev_sc[8 * b + 7:8 * b + 8, :], prev)
  for b in range(nb):
    prev_sc[8 * b:8 * (b + 1), :] = p[(b + 1) * tm - 8:(b + 1) * tm, :]
  xs = p + (prev - p) * mu_ref[...]
  r = xs[:, :RW_DIM]
  k = xs[:, RW_DIM:2 * RW_DIM]
  v = xs[:, 2 * RW_DIM:3 * RW_DIM]
  lora_in = xs[:, 3 * RW_DIM:3 * RW_DIM + LANES]
  xg = xs[:, 3 * RW_DIM + LANES:]

  w_log = -_softplus(-(w0_ref[...] + _dot(jnp.tanh(lora_in), w2_ref[...]))) - 0.5
  logw = -jnp.exp(w_log)
  a = _sigmoid(a0_ref[...] + _dot(lora_in, a2_ref[...]))
  gate = _dot(_sigmoid(xg), g2_ref[...])
  seg = seg_ref[...]
  kk = k * kk_ref[...]
  kk = kk / jnp.maximum(jnp.sqrt(_dot_exact_rhs(kk * kk, seg)), 1e-12)
  km = k * (1.0 + (a - 1.0) * ka_ref[...])
  b_s = kk * a

  g_in = _dot_exact_lhs(cum_ref[...], logw)
  g_end = _dot_exact_lhs(all_ref[...], logw)
  e_in = jnp.exp(g_in)
  e_neg = jnp.exp(-g_in)
  e_end = jnp.exp(g_end - g_in)
  at = -kk * jnp.exp(g_in - logw)
  bt = b_s * e_neg
  kt = km * e_neg
  rt = r * e_in
  bh = b_s * e_end
  kh = km * e_end
  e_tot = jnp.exp(g_end)

  lane = lax.broadcasted_iota(jnp.int32, (1, LANES), 1)
  first = lane < RW_HEAD
  row = lax.broadcasted_iota(jnp.int32, (LANES, LANES), 0)
  col = lax.broadcasted_iota(jnp.int32, (LANES, LANES), 1)
  same_head = (row // RW_CHUNK) == (col // RW_CHUNK)
  strict = same_head & (col < row)
  incl = same_head & (col <= row)

  def stack(x):
    return jnp.concatenate([jnp.where(first, x, 0.0), jnp.where(first, 0.0, x)], axis=0)

  def dup(x):
    return jnp.concatenate([x, x], axis=0)

  n_pairs = RW_HEADS // 2
  units = [(c, b, pr) for c in range(tm // RW_CHUNK) for b in range(nb) for pr in range(n_pairs)]

  def tile_of(x, c, b, pr):
    r0 = b * tm + RW_CHUNK * c
    return x[r0:r0 + RW_CHUNK, LANES * pr:LANES * (pr + 1)]

  sa = [stack(tile_of(at, *u)).astype(BF16) for u in units]
  sr = [stack(tile_of(rt, *u)).astype(BF16) for u in units]
  sv = [stack(tile_of(v, *u)).astype(BF16) for u in units]
  sc = [_dot_nt(jnp.concatenate([a_, r_], axis=0),
                jnp.concatenate([dup(tile_of(bt, *u)), dup(tile_of(kt, *u))], axis=0))
        for a_, r_, u in zip(sa, sr, units)]
  l_ab = [jnp.where(strict, x[:LANES, :LANES], 0.0) for x in sc]
  l_ak = [jnp.where(strict, x[:LANES, LANES:], 0.0).astype(BF16) for x in sc]
  l_rb = [jnp.where(incl, x[LANES:, :LANES], 0.0).astype(BF16) for x in sc]
  l_rk = [jnp.where(incl, x[LANES:, LANES:], 0.0).astype(BF16) for x in sc]
  bk_t = [jnp.concatenate([stack(tile_of(bh, *u)), stack(tile_of(kh, *u))], axis=0).T.astype(BF16)
          for u in units]
  dec = [jnp.broadcast_to(tile_of(e_tot, *u)[0:1, :], (LANES, LANES)).T for u in units]
  tinv = [t.astype(BF16) for t in _unit_lower_inverses(l_ab, row, col)]
  w1r = [jnp.concatenate([_dot(t, a_).astype(BF16), r_], axis=0) for t, a_, r_ in zip(tinv, sa, sr)]
  lakv = [_dot(l, v_) for l, v_ in zip(l_ak, sv)]
  w2 = [_dot(t, x) for t, x in zip(tinv, lakv)]
  yv = [_dot(l, v_) for l, v_ in zip(l_rk, sv)]

  n_states = nb * n_pairs
  states = [state_sc[i] for i in range(n_states)]
  for c in range(tm // RW_CHUNK):
    ids = range(c * n_states, (c + 1) * n_states)
    both = [_dot(w1r[i], states[i % n_states]) for i in ids]
    us = [bo[:LANES] + w2[i] for bo, i in zip(both, ids)]
    ubs = [u.astype(BF16) for u in us]
    ys = [bo[LANES:] + _dot(l_rb[i], ub) + yv[i] for bo, ub, i in zip(both, ubs, ids)]
    upds = [_dot(bk_t[i], jnp.concatenate([ub, sv[i]], axis=0)) for ub, i in zip(ubs, ids)]
    for y, upd, i in zip(ys, upds, ids):
      _, b, pr = units[i]
      r0 = b * tm + RW_CHUNK * c
      y_sc[r0:r0 + RW_CHUNK, LANES * pr:LANES * (pr + 1)] = y[:RW_CHUNK] + y[RW_CHUNK:]
      states[i % n_states] = states[i % n_states] * dec[i] + upd
  for i in range(n_states):
    state_sc[i] = states[i]

  y = y_sc[...]
  inv_n = 1.0 / RW_HEAD
  mean = _dot_exact_rhs(y, seg) * inv_n
  d = y - mean
  var = _dot_exact_rhs(d * d, seg) * inv_n
  yn = d * lax.rsqrt(var + RW_LN_EPS) * lng_ref[...] + lnb_ref[...]
  bonus = _dot_exact_rhs(r * km * rk_ref[...], seg) * v
  y_ref[...] = ((yn + bonus) * gate).astype(BF16).reshape(nb, tm, RW_DIM)


def _rwkv(prw, batch, seq, mu, w0, w2, a0, a2, g2, k_k, k_a, r_k, ln_g, ln_b):
  tm = min(RW_TILE, seq)
  rows = batch * tm
  zeros = jnp.zeros((RW_DECAY_LORA, RW_DIM), F32)
  w2e = jnp.concatenate([w2, zeros], axis=0).astype(BF16)
  a2e = jnp.concatenate([zeros, a2], axis=0).astype(BF16)
  idx = jnp.arange(RW_DIM) // RW_HEAD
  seg = (idx[:, None] == idx[None, :]).astype(BF16)
  tok = jnp.arange(rows)
  same_chunk = (tok[:, None] // RW_CHUNK) == (tok[None, :] // RW_CHUNK)
  cum = (same_chunk & (tok[None, :] <= tok[:, None])).astype(BF16)
  allc = same_chunk.astype(BF16)
  row = lambda x: x.reshape(1, -1)
  args = (prw.reshape(batch, seq, RW_IN), row(mu), row(w0), w2e, row(a0), a2e, g2.astype(BF16),
          row(k_k), row(k_a), row(r_k), row(ln_g), row(ln_b), seg, cum, allc)
  in_specs = [pl.BlockSpec((batch, tm, RW_IN), lambda i: (0, i, 0))]
  in_specs += [_resident(a) for a in args[1:]]
  out = pl.pallas_call(
      functools.partial(_rwkv_kernel, tm=tm, nb=batch),
      grid=(seq // tm,),
      in_specs=in_specs,
      out_specs=pl.BlockSpec((batch, tm, RW_DIM), lambda i: (0, i, 0)),
      out_shape=jax.ShapeDtypeStruct((batch, seq, RW_DIM), BF16),
      scratch_shapes=[pltpu.VMEM((8 * batch, RW_IN), F32),
                      pltpu.VMEM((batch * (RW_HEADS // 2), LANES, LANES), F32),
                      pltpu.VMEM((rows, RW_DIM), F32)],
      compiler_params=_cparams("arbitrary"),
      name="rwkv",
  )(*args)
  return out.reshape(batch * seq, RW_DIM)


def _memkv_kernel(mem_ref, g_ref, w_ref, kg_ref, k_ref, v_ref):
  xn = _rms(mem_ref[...], g_ref[...]).astype(BF16)
  kv = jnp.dot(xn, w_ref[...], preferred_element_type=F32)
  for hh in range(X_HEADS):
    sl = slice(X_HEAD * hh, X_HEAD * (hh + 1))
    k_ref[:, sl] = _rms(kv[:, sl], kg_ref[...]).astype(BF16)
  v_ref[...] = kv[:, X_DIM:].astype(BF16)


def _memkv(mem2, batch, g, wkv, k_hnorm):
  args = (mem2, g.reshape(1, -1), wkv.astype(BF16), k_hnorm.reshape(1, -1))
  return pl.pallas_call(
      _memkv_kernel,
      grid=(batch,),
      in_specs=[_rows(MEM_LEN, D_MODEL)] + [_resident(a) for a in args[1:]],
      out_specs=[_rows(MEM_LEN, X_DIM)] * 2,
      out_shape=[jax.ShapeDtypeStruct((batch * MEM_LEN, X_DIM), BF16)] * 2,
      compiler_params=_cparams("parallel"),
      name="memkv",
  )(*args)


def _xattn_block(h, k_ref, v_ref, g_ref, wq_ref, qg_ref, wo_ref):
  xn = _rms(h, g_ref[...]).astype(BF16)
  q = jnp.dot(xn, wq_ref[...], preferred_element_type=F32)
  scale = X_HEAD ** -0.5
  outs = []
  for hh in range(X_HEADS):
    sl = slice(X_HEAD * hh, X_HEAD * (hh + 1))
    qh = _rms(q[:, sl], qg_ref[...]) * scale
    s = _dot_nt(qh, k_ref[:, sl])
    e = jnp.exp(s - jnp.max(s, axis=-1, keepdims=True))
    p = e / jnp.sum(e, axis=-1, keepdims=True)
    outs.append(jnp.dot(p.astype(BF16), v_ref[:, sl], preferred_element_type=F32))
  att = jnp.concatenate(outs, axis=-1).astype(BF16)
  return h + jnp.dot(att, wo_ref[...], preferred_element_type=F32)


FFN_CHUNK = 256


def _ffn_block(h, g_ref, w1_ref, w3_ref, w2_ref):
  xn = _rms(h, g_ref[...]).astype(BF16)
  acc = h
  for c in range(FFN_HIDDEN // FFN_CHUNK):
    sl = slice(FFN_CHUNK * c, FFN_CHUNK * (c + 1))
    u1 = jnp.dot(xn, w1_ref[:, sl], preferred_element_type=F32)
    u3 = jnp.dot(xn, w3_ref[:, sl], preferred_element_type=F32)
    act = (_silu(u1) * u3).astype(BF16)
    acc = acc + jnp.dot(act, w2_ref[sl, :], preferred_element_type=F32)
  return acc


def _tail_kernel(*refs, n):
  h_ref, y_refs, (k_ref, v_ref), w_refs = refs[0], refs[1:1 + n], refs[1 + n:3 + n], refs[3 + n:3 + 2 * n]
  xa_g, wq, qg, wo, ffn_g, w1, w3, w2, o_ref = refs[3 + 2 * n:]
  h = h_ref[...]
  for y_ref, w_ref in zip(y_refs, w_refs):
    h = h + jnp.dot(y_ref[...], w_ref[...], preferred_element_type=F32)
  h = _xattn_block(h, k_ref, v_ref, xa_g, wq, qg, wo)
  o_ref[...] = _ffn_block(h, ffn_g, w1, w3, w2)


def _tail(h, ys, ws, kmem, vmem, seq, xa_g, xa_wq, xa_q_hnorm, xa_wo, ffn_g, ffn_w13, ffn_w2):
  t = h.shape[0]
  tm = min(TOKEN_TILE, seq)
  per_batch = seq // tm
  n = len(ys)
  consts = (xa_g.reshape(1, -1), xa_wq.astype(BF16), xa_q_hnorm.reshape(1, -1), xa_wo.astype(BF16),
            ffn_g.reshape(1, -1), ffn_w13[:, :FFN_HIDDEN].astype(BF16),
            ffn_w13[:, FFN_HIDDEN:].astype(BF16), ffn_w2.astype(BF16))
  mem_spec = pl.BlockSpec((MEM_LEN, X_DIM), lambda i: (i // per_batch, 0))
  in_specs = ([_rows(tm, D_MODEL)] + [_rows(tm, y.shape[1]) for y in ys] + [mem_spec, mem_spec] +
              [_resident(a) for a in (*ws, *consts)])
  return pl.pallas_call(
      functools.partial(_tail_kernel, n=n),
      grid=(t // tm,),
      in_specs=in_specs,
      out_specs=_rows(tm, D_MODEL),
      out_shape=jax.ShapeDtypeStruct((t, D_MODEL), F32),
      compiler_params=_cparams("parallel"),
      name="tail",
  )(h, *ys, kmem, vmem, *ws, *consts)


def _odd_in_kernel(h_ref, g_ref, wz_ref, wx_ref, wdt_ref, cw_ref, cb_ref, z_ref, xbc_ref, dt_ref,
                   xbuf_sc, *, tm, tiles_per_seq):
  @pl.when(pl.program_id(0) % tiles_per_seq == 0)
  def _():
    xbuf_sc[0:8, :] = jnp.zeros((8, SSM_CONV_DIM), F32)

  xn = _rms(h_ref[...], g_ref[...]).astype(BF16)
  dt_ref[...] = jnp.dot(xn, wdt_ref[...], preferred_element_type=F32)
  n_blocks = SSM_CONV_DIM // CONV_COLS

  def project(c):
    cs = slice(CONV_COLS * c, CONV_COLS * (c + 1))
    xbuf_sc[8:8 + tm, cs] = jnp.dot(xn, wx_ref[:, cs], preferred_element_type=F32)

  project(0)
  for c in range(n_blocks):
    cs = slice(CONV_COLS * c, CONV_COLS * (c + 1))
    if c + 1 < n_blocks:
      project(c + 1)
    xe = xbuf_sc[:, cs]
    acc = cw_ref[0:1, cs] * xe
    for j in range(1, SSM_CONV):
      acc = cw_ref[j:j + 1, cs] * xe + pltpu.roll(acc, 1, 0)
    xbuf_sc[0:8, cs] = xe[tm:tm + 8, :]
    xbc_ref[:, cs] = _silu(acc[8:, :] + cb_ref[:, cs]).astype(BF16)
    zs = slice(CONV_COLS * c, min(CONV_COLS * (c + 1), SSM_INNER))
    if zs.start < SSM_INNER:
      z_ref[:, zs] = jnp.dot(xn, wz_ref[:, zs], preferred_element_type=F32).astype(BF16)


def _odd_in(h, seq, g, w_in, conv_w, conv_b):
  t = h.shape[0]
  tm = min(TOKEN_TILE, seq)
  wz = w_in[:, :SSM_INNER].astype(BF16)
  wx = w_in[:, SSM_INNER:SSM_INNER + SSM_CONV_DIM].astype(BF16)
  wdt = w_in[:, SSM_INNER + SSM_CONV_DIM:]
  wdt = jnp.concatenate([wdt, jnp.zeros((D_MODEL, LANES - SSM_HEADS), F32)], axis=1).astype(BF16)
  args = (h, g.reshape(1, -1), wz, wx, wdt, conv_w, conv_b.reshape(1, -1))
  return pl.pallas_call(
      functools.partial(_odd_in_kernel, tm=tm, tiles_per_seq=seq // tm),
      grid=(t // tm,),
      in_specs=[_rows(tm, D_MODEL)] + [_resident(a) for a in args[1:]],
      out_specs=[_rows(tm, SSM_INNER), _rows(tm, SSM_CONV_DIM), _rows(tm, LANES)],
      out_shape=[jax.ShapeDtypeStruct((t, SSM_INNER), BF16),
                 jax.ShapeDtypeStruct((t, SSM_CONV_DIM), BF16),
                 jax.ShapeDtypeStruct((t, LANES), F32)],
      scratch_shapes=[pltpu.VMEM((tm + 8, SSM_CONV_DIM), F32)],
      compiler_params=_cparams("arbitrary"),
      name="odd_in",
  )(*args)


def _ssd_kernel(xbc_ref, z_ref, dt_ref, dtb_ref, alog_ref, dskip_ref, gn_ref, exp_ref, tri_ref,
                y_ref, state_sc, yacc_sc, *, L):
  @pl.when(pl.program_id(1) == 0)
  def _():
    state_sc[...] = jnp.zeros_like(state_sc)

  gn_w = SSM_GROUPS * SSM_STATE
  x = xbc_ref[:, :SSM_INNER]
  b_all = xbc_ref[:, SSM_INNER:SSM_INNER + gn_w]
  c_all = xbc_ref[:, SSM_INNER + gn_w:]

  dt = _softplus(dt_ref[...] + dtb_ref[...])
  a2 = dt * (-jnp.exp(alog_ref[...]) * LOG2E)
  cum = _dot_exact_lhs(tri_ref[...], a2)
  last = cum[L - 1:L, :]
  key_t = (cum - jnp.log2(dt)).T
  expand = exp_ref[...]
  ecum_x = _dot(jnp.exp2(cum), expand)
  wend_x = _dot(dt * jnp.exp2(last - cum), expand)
  elast_x = _dot_exact_rhs(jnp.broadcast_to(jnp.exp2(last), (8, LANES)), expand)[0:1, :]
  xend = (x * wend_x).astype(BF16)

  row = lax.broadcasted_iota(jnp.int32, (L, L), 0)
  col = lax.broadcasted_iota(jnp.int32, (L, L), 1)
  causal = col <= row
  lane = lax.broadcasted_iota(jnp.int32, (1, LANES), 1)
  first = lane < SSM_HEAD
  zero = jnp.zeros((), BF16)

  e_per = SSM_HEADS // SSM_GROUPS
  for g in range(SSM_GROUPS):
    gs = slice(SSM_GROUP_W * g, SSM_GROUP_W * (g + 1))
    bg = b_all[:, SSM_STATE * g:SSM_STATE * (g + 1)]
    cg = c_all[:, SSM_STATE * g:SSM_STATE * (g + 1)]
    cb = _dot_nt(cg, bg)
    st = state_sc[:, gs]
    yacc_sc[:, gs] = _dot(cg, st) * ecum_x[:, gs]
    for pr in range(e_per // 2):
      ps = slice(SSM_GROUP_W * g + LANES * pr, SSM_GROUP_W * g + LANES * (pr + 1))
      xp = x[:, ps]
      y_pair = jnp.zeros((L, LANES), F32)
      for j in range(2):
        hd = e_per * g + 2 * pr + j
        decay = jnp.exp2(cum[:, hd:hd + 1] - key_t[hd:hd + 1, :])
        m = jnp.where(causal, cb * decay, 0.0).astype(BF16)
        xj = jnp.where(first, xp, zero) if j == 0 else jnp.where(first, zero, xp)
        y_pair = y_pair + jnp.dot(m, xj, preferred_element_type=F32)
      yacc_sc[:, ps] += y_pair
    state_sc[:, gs] = st * elast_x[:, gs] + _dot_tn(bg, xend[:, gs])

  y = yacc_sc[...] + x * dskip_ref[...]
  y = y * _silu(z_ref[...].astype(F32))
  for g in range(SSM_GROUPS):
    gs = slice(SSM_GROUP_W * g, SSM_GROUP_W * (g + 1))
    y_ref[:, gs] = _rms(y[:, gs], gn_ref[:, gs]).astype(BF16)


def _ssd(z, xbc, dt, batch, seq, dt_bias, a_log, d_skip, gnorm):
  L = min(SSM_CHUNK, seq)
  nc = seq // L
  pad = lambda x: jnp.concatenate([x, jnp.zeros((LANES - SSM_HEADS,), F32)]).reshape(1, LANES)
  head_of_lane = jnp.arange(SSM_INNER) // SSM_HEAD
  expand = (jnp.arange(LANES)[:, None] == head_of_lane[None, :]).astype(BF16)
  tok = jnp.arange(L)
  tri = (tok[None, :] <= tok[:, None]).astype(BF16)
  args = (xbc, z, dt, pad(dt_bias), pad(a_log), jnp.repeat(d_skip, SSM_HEAD).reshape(1, -1),
          gnorm.reshape(1, -1), expand, tri)
  tiles = lambda w: pl.BlockSpec((L, w), lambda b, i: (b * nc + i, 0))
  return pl.pallas_call(
      functools.partial(_ssd_kernel, L=L),
      grid=(batch, nc),
      in_specs=[tiles(SSM_CONV_DIM), tiles(SSM_INNER), tiles(LANES)] + [_resident(a) for a in args[3:]],
      out_specs=tiles(SSM_INNER),
      out_shape=jax.ShapeDtypeStruct((batch * seq, SSM_INNER), BF16),
      scratch_shapes=[pltpu.VMEM((SSM_STATE, SSM_INNER), F32),
                      pltpu.VMEM((L, SSM_INNER), F32)],
      compiler_params=_cparams("parallel", "arbitrary"),
      name="ssd",
  )(*args)


def kernel(x, mem, positions, ev_norm, ev_w_in, mla_q_norm, mla_w_uq, mla_kv_norm, mla_w_ukv, mla_q_hnorm, mla_k_hnorm, rw_mu, rw_w0, rw_w2, rw_a0, rw_a2, rw_g2, rw_k_k, rw_k_a, rw_r_k, rw_ln_g, rw_ln_b, ev_w_out, od_norm, od_w_in, ssm_conv_w, ssm_conv_b, ssm_dt_bias, ssm_a_log, ssm_d, ssm_gnorm, od_w_out, xa_norm_x, xa_norm_mem, xa_wq, xa_wkv, xa_q_hnorm, xa_k_hnorm, xa_wo, ffn_norm, ffn_w13, ffn_w2):
  batch, seq, _ = x.shape
  depth = xa_wq.shape[0]
  h = x.reshape(batch * seq, D_MODEL)
  mem2 = mem.reshape(batch * MEM_LEN, D_MODEL)
  pos = positions.reshape(batch * seq)
  for i in range(depth):
    j = i // 2
    kmem, vmem = _memkv(mem2, batch, xa_norm_mem[i], xa_wkv[i], xa_k_hnorm[i])
    if i % 2 == 0:
      q, k, v, prw = _even_in(h, pos, ev_norm[j], ev_w_in[j], mla_q_norm[j], mla_w_uq[j],
                              mla_kv_norm[j], mla_w_ukv[j], mla_q_hnorm[j], mla_k_hnorm[j])
      y_mla = _flash(q, k, v, batch, seq)
      y_rw = _rwkv(prw, batch, seq, rw_mu[j], rw_w0[j], rw_w2[j], rw_a0[j], rw_a2[j], rw_g2[j],
                   rw_k_k[j], rw_k_a[j], rw_r_k[j].reshape(-1), rw_ln_g[j], rw_ln_b[j])
      w_out = ev_w_out[j]
      w_mla = w_out[:MLA_HEADS * MLA_V].reshape(MLA_HEADS, MLA_V, D_MODEL)
      w_mla = jnp.concatenate([w_mla, jnp.zeros((MLA_HEADS, LANES - MLA_V, D_MODEL), F32)], axis=1)
      w_mla = w_mla.reshape(MLA_HEADS * LANES, D_MODEL).astype(BF16)
      ys, ws = [y_mla, y_rw], [w_mla, w_out[MLA_HEADS * MLA_V:].astype(BF16)]
    else:
      z, xbc, dt = _odd_in(h, seq, od_norm[j], od_w_in[j], ssm_conv_w[j], ssm_conv_b[j])
      y = _ssd(z, xbc, dt, batch, seq, ssm_dt_bias[j], ssm_a_log[j], ssm_d[j], ssm_gnorm[j])
      ys, ws = [y], [od_w_out[j].astype(BF16)]
    h = _tail(h, ys, ws, kmem, vmem, seq, xa_norm_x[i], xa_wq[i], xa_q_hnorm[i], xa_wo[i],
              ffn_norm[i], ffn_w13[i], ffn_w2[i])
  return h.reshape(batch, seq, D_MODEL)
```

```python
import functools
import math

import jax
import jax.numpy as jnp
from jax import lax
from jax.experimental import pallas as pl
from jax.experimental.pallas import tpu as pltpu

F32 = jnp.float32
BF16 = jnp.bfloat16

D_MODEL = 1024
MEM_LEN = 256
NORM_EPS = 1e-6

MLA_HEADS = 8
MLA_NOPE = 64
MLA_ROPE = 32
MLA_QK = MLA_NOPE + MLA_ROPE
MLA_V = 64
MLA_Q_RANK = 384
MLA_KV_RANK = 256
ROPE_BASE = 10000.0
MLA_IN = MLA_Q_RANK + MLA_KV_RANK + MLA_ROPE

RW_HEADS = 8
RW_HEAD = 64
RW_DIM = RW_HEADS * RW_HEAD
RW_DECAY_LORA = 64
RW_AAA_LORA = 64
RW_GATE_LORA = 128
RW_LN_EPS = 64e-5
RW_IN = 3 * RW_DIM + RW_DECAY_LORA + RW_AAA_LORA + RW_GATE_LORA
RW_CHUNK = 64

SSM_INNER = 2 * D_MODEL
SSM_HEAD = 64
SSM_HEADS = SSM_INNER // SSM_HEAD
SSM_GROUPS = 4
SSM_STATE = 128
SSM_CONV = 4
SSM_CHUNK = 256
SSM_CONV_DIM = SSM_INNER + 2 * SSM_GROUPS * SSM_STATE
SSM_GROUP_W = SSM_INNER // SSM_GROUPS

X_HEADS = 4
X_HEAD = 128
X_DIM = X_HEADS * X_HEAD

FFN_HIDDEN = -((-8 * D_MODEL) // (3 * 256)) * 256

LANES = 128
V7X_VMEM_BYTES = 64 * 1024 * 1024
VMEM_LIMIT = V7X_VMEM_BYTES - 8 * 1024 * 1024

TOKEN_TILE = 512
RW_TILE = 128
FLASH_TILE = 1024
FLASH_UNROLL = 4
CONV_COLS = 512
NEG_BIG = -1e30
LOG2E = 1.4426950408889634


def _cparams(*sem):
  return pltpu.CompilerParams(dimension_semantics=sem, vmem_limit_bytes=VMEM_LIMIT)


def _resident(arr):
  nd = arr.ndim
  return pl.BlockSpec(arr.shape, lambda *_: (0,) * nd, pipeline_mode=pl.Buffered(1))


def _rows(tm, width):
  return pl.BlockSpec((tm, width), lambda i: (i, 0))


def _rms(x, g, eps=NORM_EPS):
  return x * lax.rsqrt(jnp.mean(x * x, axis=-1, keepdims=True) + eps) * g


def _dot(a, b):
  return jnp.dot(a.astype(BF16), b.astype(BF16), preferred_element_type=F32)


def _dot_nt(a, b):
  return lax.dot_general(a.astype(BF16), b.astype(BF16), (((1,), (1,)), ((), ())),
                         preferred_element_type=F32)


def _dot_tn(a, b):
  return lax.dot_general(a.astype(BF16), b.astype(BF16), (((0,), (0,)), ((), ())),
                         preferred_element_type=F32)


def _split(x):
  hi = x.astype(BF16)
  lo = (x - hi.astype(F32)).astype(BF16)
  return hi, lo


def _dot_exact_lhs(m, x):
  hi, lo = _split(x)
  return (jnp.dot(m, hi, preferred_element_type=F32) +
          jnp.dot(m, lo, preferred_element_type=F32))


def _dot_exact_rhs(x, m):
  hi, lo = _split(x)
  return (jnp.dot(hi, m, preferred_element_type=F32) +
          jnp.dot(lo, m, preferred_element_type=F32))


def _sigmoid(x):
  return 1.0 / (1.0 + jnp.exp2(x * (-LOG2E)))


def _silu(x):
  return x * _sigmoid(x)


def _softplus(x):
  return jnp.maximum(x, 0.0) + jnp.log(1.0 + jnp.exp(-jnp.abs(x)))


def _even_in_kernel(h_ref, pos_ref, g_ref, wa_ref, wrw_ref, qn_ref, wq_ref, kvn_ref, wk_ref,
                    wv_ref, gq_ref, gk_ref, invf_ref, q_ref, k_ref, v_ref, prw_ref):
  xn = _rms(h_ref[...], g_ref[...]).astype(BF16)
  pa = jnp.dot(xn, wa_ref[...], preferred_element_type=F32)
  cq = _rms(pa[:, :MLA_Q_RANK], qn_ref[...]).astype(BF16)
  ckv = _rms(pa[:, MLA_Q_RANK:MLA_Q_RANK + MLA_KV_RANK], kvn_ref[...]).astype(BF16)
  kr = pa[:, MLA_Q_RANK + MLA_KV_RANK:]

  lane = lax.broadcasted_iota(jnp.int32, (1, LANES), 1)
  ang = pos_ref[...].astype(F32) * invf_ref[...]
  cs = jnp.where(lane < MLA_NOPE, 1.0, jnp.where(lane < MLA_QK, jnp.cos(ang), jnp.sin(ang)))
  in_head = lane < MLA_QK

  zq = jnp.dot(cq, wq_ref[...], preferred_element_type=F32)
  zk = jnp.dot(ckv, wk_ref[...], preferred_element_type=F32)
  zv = jnp.dot(ckv, wv_ref[...], preferred_element_type=F32)

  def head_norm_rope(z, g_ext):
    ms = jnp.sum(jnp.where(in_head, z * z, 0.0), axis=-1, keepdims=True) * (1.0 / MLA_QK)
    zn = z * lax.rsqrt(ms + NORM_EPS) * g_ext * cs
    rot = pltpu.roll(zn, LANES - MLA_ROPE, 1)
    return jnp.where(lane < MLA_NOPE, zn, jnp.where(in_head, zn + rot, 0.0))

  q_scale = MLA_QK ** -0.5 * LOG2E
  rw_cols = 2 * LANES
  for hh in range(MLA_HEADS):
    sl = slice(LANES * hh, LANES * (hh + 1))
    q_ref[:, sl] = (head_norm_rope(zq[:, sl], gq_ref[...]) * q_scale).astype(BF16)
    k_ref[:, sl] = head_norm_rope(zk[:, sl] + kr, gk_ref[...]).astype(BF16)
    v_ref[:, sl] = jnp.where(lane == MLA_V, 1.0, zv[:, sl]).astype(BF16)
    if hh < RW_IN // rw_cols:
      rws = slice(rw_cols * hh, rw_cols * (hh + 1))
      prw_ref[:, rws] = jnp.dot(xn, wrw_ref[:, rws], preferred_element_type=F32)


def _rot_cols(w):
  half = MLA_ROPE // 2
  return jnp.concatenate([-w[..., half:], w[..., :half]], axis=-1)


def _swap_halves(g):
  half = MLA_ROPE // 2
  return jnp.concatenate([g[..., half:], g[..., :half]], axis=-1)


def _even_in(h, pos, norm, w_in, q_norm, w_uq, kv_norm, w_ukv, q_hnorm, k_hnorm):
  t = h.shape[0]
  tm = min(TOKEN_TILE, t)
  w_cq = w_in[:, :MLA_Q_RANK]
  w_ckv = w_in[:, MLA_Q_RANK:MLA_Q_RANK + MLA_KV_RANK]
  w_kr = w_in[:, MLA_Q_RANK + MLA_KV_RANK:MLA_IN]
  w_kr_ext = jnp.concatenate([jnp.zeros((D_MODEL, MLA_NOPE), F32), w_kr, _rot_cols(w_kr)], axis=1)
  wa = jnp.concatenate([w_cq, w_ckv, w_kr_ext], axis=1).astype(BF16)
  wrw = w_in[:, MLA_IN:].astype(BF16)

  wq3 = w_uq.reshape(MLA_Q_RANK, MLA_HEADS, MLA_QK)
  wq_ext = jnp.concatenate([wq3, _rot_cols(wq3[..., MLA_NOPE:])], axis=-1)
  wq_ext = wq_ext.reshape(MLA_Q_RANK, MLA_HEADS * LANES).astype(BF16)
  wkv3 = w_ukv.reshape(MLA_KV_RANK, MLA_HEADS, MLA_NOPE + MLA_V)
  pad = jnp.zeros((MLA_KV_RANK, MLA_HEADS, LANES - MLA_NOPE), F32)
  wk_ext = jnp.concatenate([wkv3[..., :MLA_NOPE], pad], axis=-1)
  wk_ext = wk_ext.reshape(MLA_KV_RANK, MLA_HEADS * LANES).astype(BF16)
  wv_ext = jnp.concatenate([wkv3[..., MLA_NOPE:], pad], axis=-1)
  wv_ext = wv_ext.reshape(MLA_KV_RANK, MLA_HEADS * LANES).astype(BF16)

  def g_ext(g):
    return jnp.concatenate([g, _swap_halves(g[MLA_NOPE:])]).reshape(1, LANES)

  half = MLA_ROPE // 2
  inv_freq = ROPE_BASE ** (-jnp.arange(half, dtype=F32) / half)
  invf = jnp.concatenate([jnp.zeros((MLA_NOPE,), F32)] + [inv_freq] * 4).reshape(1, LANES)

  args = (h, pos.reshape(t, 1), norm.reshape(1, -1), wa, wrw, q_norm.reshape(1, -1), wq_ext,
          kv_norm.reshape(1, -1), wk_ext, wv_ext, g_ext(q_hnorm), g_ext(k_hnorm), invf)
  in_specs = [_rows(tm, D_MODEL), _rows(tm, 1)] + [_resident(a) for a in args[2:]]
  hw = MLA_HEADS * LANES
  return pl.pallas_call(
      _even_in_kernel,
      grid=(t // tm,),
      in_specs=in_specs,
      out_specs=[_rows(tm, hw), _rows(tm, hw), _rows(tm, hw), _rows(tm, RW_IN)],
      out_shape=[jax.ShapeDtypeStruct((t, hw), BF16)] * 3 + [jax.ShapeDtypeStruct((t, RW_IN), F32)],
      compiler_params=_cparams("parallel"),
      name="even_in",
  )(*args)


M_LANE = MLA_V + 1


def _flash_kernel(q_ref, k_ref, v_ref, o_ref, s_sc, acc_sc, *, tile, nq):
  lane = lax.broadcasted_iota(jnp.int32, (1, LANES), 1)
  row = lax.broadcasted_iota(jnp.int32, (tile, tile), 0)
  col = lax.broadcasted_iota(jnp.int32, (tile, tile), 1)
  acc_sc[...] = jnp.broadcast_to(jnp.where(lane == M_LANE, NEG_BIG, 0.0), acc_sc.shape)

  def rows_of(i):
    return pl.ds(pl.multiple_of(i * tile, tile), tile)

  def scores(qi, j):
    return _dot_nt(q_ref[rows_of(qi), :], k_ref[rows_of(j), :])

  def accumulate(qi, j, s):
    acc = acc_sc[qi]
    m = acc[:, M_LANE:M_LANE + 1]
    part = s[:, :LANES]
    for c in range(1, tile // LANES):
      part = jnp.maximum(part, s[:, LANES * c:LANES * (c + 1)])
    m_new = jnp.maximum(m, jnp.max(part, axis=-1, keepdims=True))
    p = jnp.exp2(s - m_new).astype(BF16)
    pv = jnp.dot(p, v_ref[rows_of(j), :], preferred_element_type=F32)
    acc_sc[qi] = jnp.where(lane == M_LANE, m_new, acc * jnp.exp2(m - m_new) + pv)

  def advance(qi, j):
    wrap = j + 1 >= qi
    return jnp.where(wrap, qi + 1, qi), jnp.where(wrap, 0, j + 1)

  def sweep(qi, j, slot, prefetch=True):
    qn, jn = advance(qi, j)
    if prefetch:
      s_sc[1 - slot] = scores(jnp.minimum(qn, nq - 1), jn)
    accumulate(qi, j, s_sc[slot])
    return qn, jn

  n_off = nq * (nq - 1) // 2
  if n_off:
    s_sc[0] = scores(1, 0)

    def body(_, carry):
      qi, j = carry
      for u in range(FLASH_UNROLL):
        qi, j = sweep(qi, j, u % 2)
      return qi, j

    qi, j = lax.fori_loop(0, n_off // FLASH_UNROLL, body, (jnp.int32(1), jnp.int32(0)))
    rest = n_off % FLASH_UNROLL
    for u in range(rest):
      qi, j = sweep(qi, j, u % 2, prefetch=u + 1 < rest)

  def finish(d, s):
    accumulate(d, d, jnp.where(col <= row, s, NEG_BIG))
    acc = acc_sc[d]
    o_ref[rows_of(d), :] = (acc / acc[:, MLA_V:MLA_V + 1]).astype(BF16)

  s_sc[0] = scores(0, 0)

  def diag_body(i, carry):
    d = 2 * i
    s_sc[1] = scores(d + 1, d + 1)
    finish(d, s_sc[0])
    nxt = jnp.minimum(d + 2, nq - 1)
    s_sc[0] = scores(nxt, nxt)
    finish(d + 1, s_sc[1])
    return carry

  lax.fori_loop(0, nq // 2, diag_body, 0)
  if nq % 2:
    finish(nq - 1, s_sc[0])


def _flash(q, k, v, batch, seq):
  tile = min(FLASH_TILE, seq)
  nq = seq // tile
  per_head = pl.BlockSpec((seq, LANES), lambda b, h: (b, h))
  return pl.pallas_call(
      functools.partial(_flash_kernel, tile=tile, nq=nq),
      grid=(batch, MLA_HEADS),
      in_specs=[per_head, per_head, per_head],
      out_specs=per_head,
      out_shape=jax.ShapeDtypeStruct(q.shape, BF16),
      scratch_shapes=[pltpu.VMEM((2, tile, tile), F32), pltpu.VMEM((nq, tile, LANES), F32)],
      compiler_params=_cparams("parallel", "parallel"),
      name="flash",
  )(q, k, v)


def _unit_lower_inverses(l_abs, row, col):
  eye = (row == col).astype(F32)
  same16 = (row // 16) == (col // 16)
  same32 = (row // 32) == (col // 32)
  off16 = same32 & jnp.logical_not(same16)
  x1 = [jnp.where(same16, l, 0.0).astype(BF16) for l in l_abs]
  x2 = [_dot(x, x).astype(BF16) for x in x1]
  x4 = [_dot(x, x).astype(BF16) for x in x2]
  x8 = [_dot(x, x).astype(BF16) for x in x4]
  ts = [eye + x.astype(F32) for x in x1]
  for xs in (x2, x4, x8):
    ts = [t + _dot(t, x) for t, x in zip(ts, xs)]
  for keep in (off16, jnp.logical_not(same32)):
    los = [jnp.where(keep, l, 0.0).astype(BF16) for l in l_abs]
    tbs = [t.astype(BF16) for t in ts]
    mids = [_dot(lo, tb) for lo, tb in zip(los, tbs)]
    ts = [t + _dot(tb, mid) for t, tb, mid in zip(ts, tbs, mids)]
  return ts


def _rwkv_kernel(p_ref, mu_ref, w0_ref, w2_ref, a0_ref, a2_ref, g2_ref, kk_ref, ka_ref, rk_ref,
                 lng_ref, lnb_ref, seg_ref, cum_ref, all_ref, y_ref, prev_sc, state_sc, y_sc, *, tm, nb):
  rows = nb * tm

  @pl.when(pl.program_id(0) == 0)
  def _():
    prev_sc[...] = jnp.zeros_like(prev_sc)
    state_sc[...] = jnp.zeros_like(state_sc)

  p = p_ref[...].reshape(rows, RW_IN)
  rowid = lax.broadcasted_iota(jnp.int32, (rows, 1), 0)
  prev = pltpu.roll(p, 1, 0)
  for b in range(nb):
    prev = jnp.where(rowid == b * tm, prev_sc[8 * b + 7:8 * b + 8, :], prev)
  for b in range(nb):
    prev_sc[8 * b:8 * (b + 1), :] = p[(b + 1) * tm - 8:(b + 1) * tm, :]
  xs = p + (prev - p) * mu_ref[...]
  r = xs[:, :RW_DIM]
  k = xs[:, RW_DIM:2 * RW_DIM]
  v = xs[:, 2 * RW_DIM:3 * RW_DIM]
  lora_in = xs[:, 3 * RW_DIM:3 * RW_DIM + LANES]
  xg = xs[:, 3 * RW_DIM + LANES:]

  w_log = -_softplus(-(w0_ref[...] + _dot(jnp.tanh(lora_in), w2_ref[...]))) - 0.5
  logw = -jnp.exp(w_log)
  a = _sigmoid(a0_ref[...] + _dot(lora_in, a2_ref[...]))
  gate = _dot(_sigmoid(xg), g2_ref[...])
  seg = seg_ref[...]
  kk = k * kk_ref[...]
  kk = kk / jnp.maximum(jnp.sqrt(_dot_exact_rhs(kk * kk, seg)), 1e-12)
  km = k * (1.0 + (a - 1.0) * ka_ref[...])
  b_s = kk * a

  g_in = _dot_exact_lhs(cum_ref[...], logw)
  g_end = _dot_exact_lhs(all_ref[...], logw)
  e_in = jnp.exp(g_in)
  e_neg = jnp.exp(-g_in)
  e_end = jnp.exp(g_end - g_in)
  at = -kk * jnp.exp(g_in - logw)
  bt = b_s * e_neg
  kt = km * e_neg
  rt = r * e_in
  bh = b_s * e_end
  kh = km * e_end
  e_tot = jnp.exp(g_end)

  lane = lax.broadcasted_iota(jnp.int32, (1, LANES), 1)
  first = lane < RW_HEAD
  row = lax.broadcasted_iota(jnp.int32, (LANES, LANES), 0)
  col = lax.broadcasted_iota(jnp.int32, (LANES, LANES), 1)
  same_head = (row // RW_CHUNK) == (col // RW_CHUNK)
  strict = same_head & (col < row)
  incl = same_head & (col <= row)

  def stack(x):
    return jnp.concatenate([jnp.where(first, x, 0.0), jnp.where(first, 0.0, x)], axis=0)

  def dup(x):
    return jnp.concatenate([x, x], axis=0)

  n_pairs = RW_HEADS // 2
  units = [(c, b, pr) for c in range(tm // RW_CHUNK) for b in range(nb) for pr in range(n_pairs)]

  def tile_of(x, c, b, pr):
    r0 = b * tm + RW_CHUNK * c
    return x[r0:r0 + RW_CHUNK, LANES * pr:LANES * (pr + 1)]

  sa = [stack(tile_of(at, *u)).astype(BF16) for u in units]
  sr = [stack(tile_of(rt, *u)).astype(BF16) for u in units]
  sv = [stack(tile_of(v, *u)).astype(BF16) for u in units]
  sc = [_dot_nt(jnp.concatenate([a_, r_], axis=0),
                jnp.concatenate([dup(tile_of(bt, *u)), dup(tile_of(kt, *u))], axis=0))
        for a_, r_, u in zip(sa, sr, units)]
  l_ab = [jnp.where(strict, x[:LANES, :LANES], 0.0) for x in sc]
  l_ak = [jnp.where(strict, x[:LANES, LANES:], 0.0).astype(BF16) for x in sc]
  l_rb = [jnp.where(incl, x[LANES:, :LANES], 0.0).astype(BF16) for x in sc]
  l_rk = [jnp.where(incl, x[LANES:, LANES:], 0.0).astype(BF16) for x in sc]
  bk_t = [jnp.concatenate([stack(tile_of(bh, *u)), stack(tile_of(kh, *u))], axis=0).T.astype(BF16)
          for u in units]
  dec = [jnp.broadcast_to(tile_of(e_tot, *u)[0:1, :], (LANES, LANES)).T for u in units]
  tinv = [t.astype(BF16) for t in _unit_lower_inverses(l_ab, row, col)]
  w1r = [jnp.concatenate([_dot(t, a_).astype(BF16), r_], axis=0) for t, a_, r_ in zip(tinv, sa, sr)]
  lakv = [_dot(l, v_) for l, v_ in zip(l_ak, sv)]
  w2 = [_dot(t, x) for t, x in zip(tinv, lakv)]
  yv = [_dot(l, v_) for l, v_ in zip(l_rk, sv)]

  n_states = nb * n_pairs
  states = [state_sc[i] for i in range(n_states)]
  for c in range(tm // RW_CHUNK):
    ids = range(c * n_states, (c + 1) * n_states)
    both = [_dot(w1r[i], states[i % n_states]) for i in ids]
    us = [bo[:LANES] + w2[i] for bo, i in zip(both, ids)]
    ubs = [u.astype(BF16) for u in us]
    ys = [bo[LANES:] + _dot(l_rb[i], ub) + yv[i] for bo, ub, i in zip(both, ubs, ids)]
    upds = [_dot(bk_t[i], jnp.concatenate([ub, sv[i]], axis=0)) for ub, i in zip(ubs, ids)]
    for y, upd, i in zip(ys, upds, ids):
      _, b, pr = units[i]
      r0 = b * tm + RW_CHUNK * c
      y_sc[r0:r0 + RW_CHUNK, LANES * pr:LANES * (pr + 1)] = y[:RW_CHUNK] + y[RW_CHUNK:]
      states[i % n_states] = states[i % n_states] * dec[i] + upd
  for i in range(n_states):
    state_sc[i] = states[i]

  y = y_sc[...]
  inv_n = 1.0 / RW_HEAD
  mean = _dot_exact_rhs(y, seg) * inv_n
  d = y - mean
  var = _dot_exact_rhs(d * d, seg) * inv_n
  yn = d * lax.rsqrt(var + RW_LN_EPS) * lng_ref[...] + lnb_ref[...]
  bonus = _dot_exact_rhs(r * km * rk_ref[...], seg) * v
  y_ref[...] = ((yn + bonus) * gate).astype(BF16).reshape(nb, tm, RW_DIM)


def _rwkv(prw, batch, seq, mu, w0, w2, a0, a2, g2, k_k, k_a, r_k, ln_g, ln_b):
  tm = min(RW_TILE, seq)
  rows = batch * tm
  zeros = jnp.zeros((RW_DECAY_LORA, RW_DIM), F32)
  w2e = jnp.concatenate([w2, zeros], axis=0).astype(BF16)
  a2e = jnp.concatenate([zeros, a2], axis=0).astype(BF16)
  idx = jnp.arange(RW_DIM) // RW_HEAD
  seg = (idx[:, None] == idx[None, :]).astype(BF16)
  tok = jnp.arange(rows)
  same_chunk = (tok[:, None] // RW_CHUNK) == (tok[None, :] // RW_CHUNK)
  cum = (same_chunk & (tok[None, :] <= tok[:, None])).astype(BF16)
  allc = same_chunk.astype(BF16)
  row = lambda x: x.reshape(1, -1)
  args = (prw.reshape(batch, seq, RW_IN), row(mu), row(w0), w2e, row(a0), a2e, g2.astype(BF16),
          row(k_k), row(k_a), row(r_k), row(ln_g), row(ln_b), seg, cum, allc)
  in_specs = [pl.BlockSpec((batch, tm, RW_IN), lambda i: (0, i, 0))]
  in_specs += [_resident(a) for a in args[1:]]
  out = pl.pallas_call(
      functools.partial(_rwkv_kernel, tm=tm, nb=batch),
      grid=(seq // tm,),
      in_specs=in_specs,
      out_specs=pl.BlockSpec((batch, tm, RW_DIM), lambda i: (0, i, 0)),
      out_shape=jax.ShapeDtypeStruct((batch, seq, RW_DIM), BF16),
      scratch_shapes=[pltpu.VMEM((8 * batch, RW_IN), F32),
                      pltpu.VMEM((batch * (RW_HEADS // 2), LANES, LANES), F32),
                      pltpu.VMEM((rows, RW_DIM), F32)],
      compiler_params=_cparams("arbitrary"),
      name="rwkv",
  )(*args)
  return out.reshape(batch * seq, RW_DIM)


def _memkv_kernel(mem_ref, g_ref, w_ref, kg_ref, k_ref, v_ref):
  xn = _rms(mem_ref[...], g_ref[...]).astype(BF16)
  kv = jnp.dot(xn, w_ref[...], preferred_element_type=F32)
  for hh in range(X_HEADS):
    sl = slice(X_HEAD * hh, X_HEAD * (hh + 1))
    k_ref[:, sl] = _rms(kv[:, sl], kg_ref[...]).astype(BF16)
  v_ref[...] = kv[:, X_DIM:].astype(BF16)


def _memkv(mem2, batch, g, wkv, k_hnorm):
  args = (mem2, g.reshape(1, -1), wkv.astype(BF16), k_hnorm.reshape(1, -1))
  return pl.pallas_call(
      _memkv_kernel,
      grid=(batch,),
      in_specs=[_rows(MEM_LEN, D_MODEL)] + [_resident(a) for a in args[1:]],
      out_specs=[_rows(MEM_LEN, X_DIM)] * 2,
      out_shape=[jax.ShapeDtypeStruct((batch * MEM_LEN, X_DIM), BF16)] * 2,
      compiler_params=_cparams("parallel"),
      name="memkv",
  )(*args)


def _xattn_block(h, k_ref, v_ref, g_ref, wq_ref, qg_ref, wo_ref):
  xn = _rms(h, g_ref[...]).astype(BF16)
  q = jnp.dot(xn, wq_ref[...], preferred_element_type=F32)
  scale = X_HEAD ** -0.5
  outs = []
  for hh in range(X_HEADS):
    sl = slice(X_HEAD * hh, X_HEAD * (hh + 1))
    qh = _rms(q[:, sl], qg_ref[...]) * scale
    s = _dot_nt(qh, k_ref[:, sl])
    e = jnp.exp(s - jnp.max(s, axis=-1, keepdims=True))
    p = e / jnp.sum(e, axis=-1, keepdims=True)
    outs.append(jnp.dot(p.astype(BF16), v_ref[:, sl], preferred_element_type=F32))
  att = jnp.concatenate(outs, axis=-1).astype(BF16)
  return h + jnp.dot(att, wo_ref[...], preferred_element_type=F32)


FFN_CHUNK = 256


def _ffn_block(h, g_ref, w1_ref, w3_ref, w2_ref):
  xn = _rms(h, g_ref[...]).astype(BF16)
  acc = h
  for c in range(FFN_HIDDEN // FFN_CHUNK):
    sl = slice(FFN_CHUNK * c, FFN_CHUNK * (c + 1))
    u1 = jnp.dot(xn, w1_ref[:, sl], preferred_element_type=F32)
    u3 = jnp.dot(xn, w3_ref[:, sl], preferred_element_type=F32)
    act = (_silu(u1) * u3).astype(BF16)
    acc = acc + jnp.dot(act, w2_ref[sl, :], preferred_element_type=F32)
  return acc


def _tail_kernel(*refs, n):
  h_ref, y_refs, (k_ref, v_ref), w_refs = refs[0], refs[1:1 + n], refs[1 + n:3 + n], refs[3 + n:3 + 2 * n]
  xa_g, wq, qg, wo, ffn_g, w1, w3, w2, o_ref = refs[3 + 2 * n:]
  h = h_ref[...]
  for y_ref, w_ref in zip(y_refs, w_refs):
    h = h + jnp.dot(y_ref[...], w_ref[...], preferred_element_type=F32)
  h = _xattn_block(h, k_ref, v_ref, xa_g, wq, qg, wo)
  o_ref[...] = _ffn_block(h, ffn_g, w1, w3, w2)


def _tail(h, ys, ws, kmem, vmem, seq, xa_g, xa_wq, xa_q_hnorm, xa_wo, ffn_g, ffn_w13, ffn_w2):
  t = h.shape[0]
  tm = min(TOKEN_TILE, seq)
  per_batch = seq // tm
  n = len(ys)
  consts = (xa_g.reshape(1, -1), xa_wq.astype(BF16), xa_q_hnorm.reshape(1, -1), xa_wo.astype(BF16),
            ffn_g.reshape(1, -1), ffn_w13[:, :FFN_HIDDEN].astype(BF16),
            ffn_w13[:, FFN_HIDDEN:].astype(BF16), ffn_w2.astype(BF16))
  mem_spec = pl.BlockSpec((MEM_LEN, X_DIM), lambda i: (i // per_batch, 0))
  in_specs = ([_rows(tm, D_MODEL)] + [_rows(tm, y.shape[1]) for y in ys] + [mem_spec, mem_spec] +
              [_resident(a) for a in (*ws, *consts)])
  return pl.pallas_call(
      functools.partial(_tail_kernel, n=n),
      grid=(t // tm,),
      in_specs=in_specs,
      out_specs=_rows(tm, D_MODEL),
      out_shape=jax.ShapeDtypeStruct((t, D_MODEL), F32),
      compiler_params=_cparams("parallel"),
      name="tail",
  )(h, *ys, kmem, vmem, *ws, *consts)


def _odd_in_kernel(h_ref, g_ref, wz_ref, wx_ref, wdt_ref, cw_ref, cb_ref, z_ref, xbc_ref, dt_ref,
                   xbuf_sc, *, tm, tiles_per_seq):
  @pl.when(pl.program_id(0) % tiles_per_seq == 0)
  def _():
    xbuf_sc[0:8, :] = jnp.zeros((8, SSM_CONV_DIM), F32)

  xn = _rms(h_ref[...], g_ref[...]).astype(BF16)
  dt_ref[...] = jnp.dot(xn, wdt_ref[...], preferred_element_type=F32)
  n_blocks = SSM_CONV_DIM // CONV_COLS

  def project(c):
    cs = slice(CONV_COLS * c, CONV_COLS * (c + 1))
    xbuf_sc[8:8 + tm, cs] = jnp.dot(xn, wx_ref[:, cs], preferred_element_type=F32)

  project(0)
  for c in range(n_blocks):
    cs = slice(CONV_COLS * c, CONV_COLS * (c + 1))
    if c + 1 < n_blocks:
      project(c + 1)
    xe = xbuf_sc[:, cs]
    acc = cw_ref[0:1, cs] * xe
    for j in range(1, SSM_CONV):
      acc = cw_ref[j:j + 1, cs] * xe + pltpu.roll(acc, 1, 0)
    xbuf_sc[0:8, cs] = xe[tm:tm + 8, :]
    xbc_ref[:, cs] = _silu(acc[8:, :] + cb_ref[:, cs]).astype(BF16)
    zs = slice(CONV_COLS * c, min(CONV_COLS * (c + 1), SSM_INNER))
    if zs.start < SSM_INNER:
      z_ref[:, zs] = jnp.dot(xn, wz_ref[:, zs], preferred_element_type=F32).astype(BF16)


def _odd_in(h, seq, g, w_in, conv_w, conv_b):
  t = h.shape[0]
  tm = min(TOKEN_TILE, seq)
  wz = w_in[:, :SSM_INNER].astype(BF16)
  wx = w_in[:, SSM_INNER:SSM_INNER + SSM_CONV_DIM].astype(BF16)
  wdt = w_in[:, SSM_INNER + SSM_CONV_DIM:]
  wdt = jnp.concatenate([wdt, jnp.zeros((D_MODEL, LANES - SSM_HEADS), F32)], axis=1).astype(BF16)
  args = (h, g.reshape(1, -1), wz, wx, wdt, conv_w, conv_b.reshape(1, -1))
  return pl.pallas_call(
      functools.partial(_odd_in_kernel, tm=tm, tiles_per_seq=seq // tm),
      grid=(t // tm,),
      in_specs=[_rows(tm, D_MODEL)] + [_resident(a) for a in args[1:]],
      out_specs=[_rows(tm, SSM_INNER), _rows(tm, SSM_CONV_DIM), _rows(tm, LANES)],
      out_shape=[jax.ShapeDtypeStruct((t, SSM_INNER), BF16),
                 jax.ShapeDtypeStruct((t, SSM_CONV_DIM), BF16),
                 jax.ShapeDtypeStruct((t, LANES), F32)],
      scratch_shapes=[pltpu.VMEM((tm + 8, SSM_CONV_DIM), F32)],
      compiler_params=_cparams("arbitrary"),
      name="odd_in",
  )(*args)


def _ssd_kernel(xbc_ref, z_ref, dt_ref, dtb_ref, alog_ref, dskip_ref, gn_ref, exp_ref, tri_ref,
                y_ref, state_sc, yacc_sc, *, L):
  @pl.when(pl.program_id(1) == 0)
  def _():
    state_sc[...] = jnp.zeros_like(state_sc)

  gn_w = SSM_GROUPS * SSM_STATE
  x = xbc_ref[:, :SSM_INNER]
  b_all = xbc_ref[:, SSM_INNER:SSM_INNER + gn_w]
  c_all = xbc_ref[:, SSM_INNER + gn_w:]

  dt = _softplus(dt_ref[...] + dtb_ref[...])
  a2 = dt * (-jnp.exp(alog_ref[...]) * LOG2E)
  cum = _dot_exact_lhs(tri_ref[...], a2)
  last = cum[L - 1:L, :]
  key_t = (cum - jnp.log2(dt)).T
  expand = exp_ref[...]
  ecum_x = _dot(jnp.exp2(cum), expand)
  wend_x = _dot(dt * jnp.exp2(last - cum), expand)
  elast_x = _dot_exact_rhs(jnp.broadcast_to(jnp.exp2(last), (8, LANES)), expand)[0:1, :]
  xend = (x * wend_x).astype(BF16)

  row = lax.broadcasted_iota(jnp.int32, (L, L), 0)
  col = lax.broadcasted_iota(jnp.int32, (L, L), 1)
  causal = col <= row
  lane = lax.broadcasted_iota(jnp.int32, (1, LANES), 1)
  first = lane < SSM_HEAD
  zero = jnp.zeros((), BF16)

  e_per = SSM_HEADS // SSM_GROUPS
  for g in range(SSM_GROUPS):
    gs = slice(SSM_GROUP_W * g, SSM_GROUP_W * (g + 1))
    bg = b_all[:, SSM_STATE * g:SSM_STATE * (g + 1)]
    cg = c_all[:, SSM_STATE * g:SSM_STATE * (g + 1)]
    cb = _dot_nt(cg, bg)
    st = state_sc[:, gs]
    yacc_sc[:, gs] = _dot(cg, st) * ecum_x[:, gs]
    for pr in range(e_per // 2):
      ps = slice(SSM_GROUP_W * g + LANES * pr, SSM_GROUP_W * g + LANES * (pr + 1))
      xp = x[:, ps]
      y_pair = jnp.zeros((L, LANES), F32)
      for j in range(2):
        hd = e_per * g + 2 * pr + j
        decay = jnp.exp2(cum[:, hd:hd + 1] - key_t[hd:hd + 1, :])
        m = jnp.where(causal, cb * decay, 0.0).astype(BF16)
        xj = jnp.where(first, xp, zero) if j == 0 else jnp.where(first, zero, xp)
        y_pair = y_pair + jnp.dot(m, xj, preferred_element_type=F32)
      yacc_sc[:, ps] += y_pair
    state_sc[:, gs] = st * elast_x[:, gs] + _dot_tn(bg, xend[:, gs])

  y = yacc_sc[...] + x * dskip_ref[...]
  y = y * _silu(z_ref[...].astype(F32))
  for g in range(SSM_GROUPS):
    gs = slice(SSM_GROUP_W * g, SSM_GROUP_W * (g + 1))
    y_ref[:, gs] = _rms(y[:, gs], gn_ref[:, gs]).astype(BF16)


def _ssd(z, xbc, dt, batch, seq, dt_bias, a_log, d_skip, gnorm):
  L = min(SSM_CHUNK, seq)
  nc = seq // L
  pad = lambda x: jnp.concatenate([x, jnp.zeros((LANES - SSM_HEADS,), F32)]).reshape(1, LANES)
  head_of_lane = jnp.arange(SSM_INNER) // SSM_HEAD
  expand = (jnp.arange(LANES)[:, None] == head_of_lane[None, :]).astype(BF16)
  tok = jnp.arange(L)
  tri = (tok[None, :] <= tok[:, None]).astype(BF16)
  args = (xbc, z, dt, pad(dt_bias), pad(a_log), jnp.repeat(d_skip, SSM_HEAD).reshape(1, -1),
          gnorm.reshape(1, -1), expand, tri)
  tiles = lambda w: pl.BlockSpec((L, w), lambda b, i: (b * nc + i, 0))
  return pl.pallas_call(
      functools.partial(_ssd_kernel, L=L),
      grid=(batch, nc),
      in_specs=[tiles(SSM_CONV_DIM), tiles(SSM_INNER), tiles(LANES)] + [_resident(a) for a in args[3:]],
      out_specs=tiles(SSM_INNER),
      out_shape=jax.ShapeDtypeStruct((batch * seq, SSM_INNER), BF16),
      scratch_shapes=[pltpu.VMEM((SSM_STATE, SSM_INNER), F32),
                      pltpu.VMEM((L, SSM_INNER), F32)],
      compiler_params=_cparams("parallel", "arbitrary"),
      name="ssd",
  )(*args)


def kernel(x, mem, positions, ev_norm, ev_w_in, mla_q_norm, mla_w_uq, mla_kv_norm, mla_w_ukv, mla_q_hnorm, mla_k_hnorm, rw_mu, rw_w0, rw_w2, rw_a0, rw_a2, rw_g2, rw_k_k, rw_k_a, rw_r_k, rw_ln_g, rw_ln_b, ev_w_out, od_norm, od_w_in, ssm_conv_w, ssm_conv_b, ssm_dt_bias, ssm_a_log, ssm_d, ssm_gnorm, od_w_out, xa_norm_x, xa_norm_mem, xa_wq, xa_wkv, xa_q_hnorm, xa_k_hnorm, xa_wo, ffn_norm, ffn_w13, ffn_w2):
  batch, seq, _ = x.shape
  depth = xa_wq.shape[0]
  h = x.reshape(batch * seq, D_MODEL)
  mem2 = mem.reshape(batch * MEM_LEN, D_MODEL)
  pos = positions.reshape(batch * seq)
  for i in range(depth):
    j = i // 2
    kmem, vmem = _memkv(mem2, batch, xa_norm_mem[i], xa_wkv[i], xa_k_hnorm[i])
    if i % 2 == 0:
      q, k, v, prw = _even_in(h, pos, ev_norm[j], ev_w_in[j], mla_q_norm[j], mla_w_uq[j],
                              mla_kv_norm[j], mla_w_ukv[j], mla_q_hnorm[j], mla_k_hnorm[j])
      y_mla = _flash(q, k, v, batch, seq)
      y_rw = _rwkv(prw, batch, seq, rw_mu[j], rw_w0[j], rw_w2[j], rw_a0[j], rw_a2[j], rw_g2[j],
                   rw_k_k[j], rw_k_a[j], rw_r_k[j].reshape(-1), rw_ln_g[j], rw_ln_b[j])
      w_out = ev_w_out[j]
      w_mla = w_out[:MLA_HEADS * MLA_V].reshape(MLA_HEADS, MLA_V, D_MODEL)
      w_mla = jnp.concatenate([w_mla, jnp.zeros((MLA_HEADS, LANES - MLA_V, D_MODEL), F32)], axis=1)
      w_mla = w_mla.reshape(MLA_HEADS * LANES, D_MODEL).astype(BF16)
      ys, ws = [y_mla, y_rw], [w_mla, w_out[MLA_HEADS * MLA_V:].astype(BF16)]
    else:
      z, xbc, dt = _odd_in(h, seq, od_norm[j], od_w_in[j], ssm_conv_w[j], ssm_conv_b[j])
      y = _ssd(z, xbc, dt, batch, seq, ssm_dt_bias[j], ssm_a_log[j], ssm_d[j], ssm_gnorm[j])
      ys, ws = [y], [od_w_out[j].astype(BF16)]
    h = _tail(h, ys, ws, kmem, vmem, seq, xa_norm_x[i], xa_wq[i], xa_q_hnorm[i], xa_wo[i],
              ffn_norm[i], ffn_w13[i], ffn_w2[i])
  return h.reshape(batch, seq, D_MODEL)
```

```python
import functools
import math

import jax
import jax.numpy as jnp
from jax import lax
from jax.experimental import pallas as pl
from jax.experimental.pallas import tpu as pltpu

F32 = jnp.float32
BF16 = jnp.bfloat16

D_MODEL = 1024
MEM_LEN = 256
NORM_EPS = 1e-6

MLA_HEADS = 8
MLA_NOPE = 64
MLA_ROPE = 32
MLA_QK = MLA_NOPE + MLA_ROPE
MLA_V = 64
MLA_Q_RANK = 384
MLA_KV_RANK = 256
ROPE_BASE = 10000.0
MLA_IN = MLA_Q_RANK + MLA_KV_RANK + MLA_ROPE

RW_HEADS = 8
RW_HEAD = 64
RW_DIM = RW_HEADS * RW_HEAD
RW_DECAY_LORA = 64
RW_AAA_LORA = 64
RW_GATE_LORA = 128
RW_LN_EPS = 64e-5
RW_IN = 3 * RW_DIM + RW_DECAY_LORA + RW_AAA_LORA + RW_GATE_LORA
RW_CHUNK = 64

SSM_INNER = 2 * D_MODEL
SSM_HEAD = 64
SSM_HEADS = SSM_INNER // SSM_HEAD
SSM_GROUPS = 4
SSM_STATE = 128
SSM_CONV = 4
SSM_CHUNK = 256
SSM_CONV_DIM = SSM_INNER + 2 * SSM_GROUPS * SSM_STATE
SSM_GROUP_W = SSM_INNER // SSM_GROUPS

X_HEADS = 4
X_HEAD = 128
X_DIM = X_HEADS * X_HEAD

FFN_HIDDEN = -((-8 * D_MODEL) // (3 * 256)) * 256

LANES = 128
V7X_VMEM_BYTES = 64 * 1024 * 1024
VMEM_LIMIT = V7X_VMEM_BYTES - 8 * 1024 * 1024

TOKEN_TILE = 512
RW_TILE = 128
FLASH_TILE = 1024
FLASH_UNROLL = 8
CONV_COLS = 512
NEG_BIG = -1e30
LOG2E = 1.4426950408889634


def _cparams(*sem):
  return pltpu.CompilerParams(dimension_semantics=sem, vmem_limit_bytes=VMEM_LIMIT)


def _resident(arr):
  nd = arr.ndim
  return pl.BlockSpec(arr.shape, lambda *_: (0,) * nd, pipeline_mode=pl.Buffered(1))


def _rows(tm, width):
  return pl.BlockSpec((tm, width), lambda i: (i, 0))


def _rms(x, g, eps=NORM_EPS):
  return x * lax.rsqrt(jnp.mean(x * x, axis=-1, keepdims=True) + eps) * g


def _dot(a, b):
  return jnp.dot(a.astype(BF16), b.astype(BF16), preferred_element_type=F32)


def _dot_nt(a, b):
  return lax.dot_general(a.astype(BF16), b.astype(BF16), (((1,), (1,)), ((), ())),
                         preferred_element_type=F32)


def _dot_tn(a, b):
  return lax.dot_general(a.astype(BF16), b.astype(BF16), (((0,), (0,)), ((), ())),
                         preferred_element_type=F32)


def _split(x):
  hi = x.astype(BF16)
  lo = (x - hi.astype(F32)).astype(BF16)
  return hi, lo


def _dot_exact_lhs(m, x):
  hi, lo = _split(x)
  return (jnp.dot(m, hi, preferred_element_type=F32) +
          jnp.dot(m, lo, preferred_element_type=F32))


def _dot_exact_rhs(x, m):
  hi, lo = _split(x)
  return (jnp.dot(hi, m, preferred_element_type=F32) +
          jnp.dot(lo, m, preferred_element_type=F32))


def _sigmoid(x):
  return 1.0 / (1.0 + jnp.exp2(x * (-LOG2E)))


def _silu(x):
  return x * _sigmoid(x)


def _softplus(x):
  return jnp.maximum(x, 0.0) + jnp.log(1.0 + jnp.exp(-jnp.abs(x)))


def _even_in_kernel(h_ref, pos_ref, g_ref, wa_ref, wrw_ref, qn_ref, wq_ref, kvn_ref, wk_ref,
                    wv_ref, gq_ref, gk_ref, invf_ref, q_ref, k_ref, v_ref, prw_ref):
  xn = _rms(h_ref[...], g_ref[...]).astype(BF16)
  pa = jnp.dot(xn, wa_ref[...], preferred_element_type=F32)
  cq = _rms(pa[:, :MLA_Q_RANK], qn_ref[...]).astype(BF16)
  ckv = _rms(pa[:, MLA_Q_RANK:MLA_Q_RANK + MLA_KV_RANK], kvn_ref[...]).astype(BF16)
  kr = pa[:, MLA_Q_RANK + MLA_KV_RANK:]

  lane = lax.broadcasted_iota(jnp.int32, (1, LANES), 1)
  ang = pos_ref[...].astype(F32) * invf_ref[...]
  cs = jnp.where(lane < MLA_NOPE, 1.0, jnp.where(lane < MLA_QK, jnp.cos(ang), jnp.sin(ang)))
  in_head = lane < MLA_QK

  zq = jnp.dot(cq, wq_ref[...], preferred_element_type=F32)
  zk = jnp.dot(ckv, wk_ref[...], preferred_element_type=F32)
  zv = jnp.dot(ckv, wv_ref[...], preferred_element_type=F32)

  def head_norm_rope(z, g_ext):
    ms = jnp.sum(jnp.where(in_head, z * z, 0.0), axis=-1, keepdims=True) * (1.0 / MLA_QK)
    zn = z * lax.rsqrt(ms + NORM_EPS) * g_ext * cs
    rot = pltpu.roll(zn, LANES - MLA_ROPE, 1)
    return jnp.where(lane < MLA_NOPE, zn, jnp.where(in_head, zn + rot, 0.0))

  q_scale = MLA_QK ** -0.5 * LOG2E
  rw_cols = 2 * LANES
  for hh in range(MLA_HEADS):
    sl = slice(LANES * hh, LANES * (hh + 1))
    q_ref[:, sl] = (head_norm_rope(zq[:, sl], gq_ref[...]) * q_scale).astype(BF16)
    k_ref[:, sl] = head_norm_rope(zk[:, sl] + kr, gk_ref[...]).astype(BF16)
    v_ref[:, sl] = jnp.where(lane == MLA_V, 1.0, zv[:, sl]).astype(BF16)
    if hh < RW_IN // rw_cols:
      rws = slice(rw_cols * hh, rw_cols * (hh + 1))
      prw_ref[:, rws] = jnp.dot(xn, wrw_ref[:, rws], preferred_element_type=F32)


def _rot_cols(w):
  half = MLA_ROPE // 2
  return jnp.concatenate([-w[..., half:], w[..., :half]], axis=-1)


def _swap_halves(g):
  half = MLA_ROPE // 2
  return jnp.concatenate([g[..., half:], g[..., :half]], axis=-1)


def _even_in(h, pos, norm, w_in, q_norm, w_uq, kv_norm, w_ukv, q_hnorm, k_hnorm):
  t = h.shape[0]
  tm = min(TOKEN_TILE, t)
  w_cq = w_in[:, :MLA_Q_RANK]
  w_ckv = w_in[:, MLA_Q_RANK:MLA_Q_RANK + MLA_KV_RANK]
  w_kr = w_in[:, MLA_Q_RANK + MLA_KV_RANK:MLA_IN]
  w_kr_ext = jnp.concatenate([jnp.zeros((D_MODEL, MLA_NOPE), F32), w_kr, _rot_cols(w_kr)], axis=1)
  wa = jnp.concatenate([w_cq, w_ckv, w_kr_ext], axis=1).astype(BF16)
  wrw = w_in[:, MLA_IN:].astype(BF16)

  wq3 = w_uq.reshape(MLA_Q_RANK, MLA_HEADS, MLA_QK)
  wq_ext = jnp.concatenate([wq3, _rot_cols(wq3[..., MLA_NOPE:])], axis=-1)
  wq_ext = wq_ext.reshape(MLA_Q_RANK, MLA_HEADS * LANES).astype(BF16)
  wkv3 = w_ukv.reshape(MLA_KV_RANK, MLA_HEADS, MLA_NOPE + MLA_V)
  pad = jnp.zeros((MLA_KV_RANK, MLA_HEADS, LANES - MLA_NOPE), F32)
  wk_ext = jnp.concatenate([wkv3[..., :MLA_NOPE], pad], axis=-1)
  wk_ext = wk_ext.reshape(MLA_KV_RANK, MLA_HEADS * LANES).astype(BF16)
  wv_ext = jnp.concatenate([wkv3[..., MLA_NOPE:], pad], axis=-1)
  wv_ext = wv_ext.reshape(MLA_KV_RANK, MLA_HEADS * LANES).astype(BF16)

  def g_ext(g):
    return jnp.concatenate([g, _swap_halves(g[MLA_NOPE:])]).reshape(1, LANES)

  half = MLA_ROPE // 2
  inv_freq = ROPE_BASE ** (-jnp.arange(half, dtype=F32) / half)
  invf = jnp.concatenate([jnp.zeros((MLA_NOPE,), F32)] + [inv_freq] * 4).reshape(1, LANES)

  args = (h, pos.reshape(t, 1), norm.reshape(1, -1), wa, wrw, q_norm.reshape(1, -1), wq_ext,
          kv_norm.reshape(1, -1), wk_ext, wv_ext, g_ext(q_hnorm), g_ext(k_hnorm), invf)
  in_specs = [_rows(tm, D_MODEL), _rows(tm, 1)] + [_resident(a) for a in args[2:]]
  hw = MLA_HEADS * LANES
  return pl.pallas_call(
      _even_in_kernel,
      grid=(t // tm,),
      in_specs=in_specs,
      out_specs=[_rows(tm, hw), _rows(tm, hw), _rows(tm, hw), _rows(tm, RW_IN)],
      out_shape=[jax.ShapeDtypeStruct((t, hw), BF16)] * 3 + [jax.ShapeDtypeStruct((t, RW_IN), F32)],
      compiler_params=_cparams("parallel"),
      name="even_in",
  )(*args)


M_LANE = MLA_V + 1


def _flash_kernel(q_ref, k_ref, v_ref, o_ref, s_sc, acc_sc, *, tile, nq):
  lane = lax.broadcasted_iota(jnp.int32, (1, LANES), 1)
  row = lax.broadcasted_iota(jnp.int32, (tile, tile), 0)
  col = lax.broadcasted_iota(jnp.int32, (tile, tile), 1)
  acc_sc[...] = jnp.broadcast_to(jnp.where(lane == M_LANE, NEG_BIG, 0.0), acc_sc.shape)

  def rows_of(i):
    return pl.ds(pl.multiple_of(i * tile, tile), tile)

  def scores(qi, j):
    return _dot_nt(q_ref[rows_of(qi), :], k_ref[rows_of(j), :])

  def accumulate(qi, j, s):
    acc = acc_sc[qi]
    m = acc[:, M_LANE:M_LANE + 1]
    part = s[:, :LANES]
    for c in range(1, tile // LANES):
      part = jnp.maximum(part, s[:, LANES * c:LANES * (c + 1)])
    m_new = jnp.maximum(m, jnp.max(part, axis=-1, keepdims=True))
    p = jnp.exp2(s - m_new).astype(BF16)
    pv = jnp.dot(p, v_ref[rows_of(j), :], preferred_element_type=F32)
    acc_sc[qi] = jnp.where(lane == M_LANE, m_new, acc * jnp.exp2(m - m_new) + pv)

  def advance(qi, j):
    wrap = j + 1 >= qi
    return jnp.where(wrap, qi + 1, qi), jnp.where(wrap, 0, j + 1)

  def sweep(qi, j, slot, prefetch=True):
    qn, jn = advance(qi, j)
    if prefetch:
      s_sc[1 - slot] = scores(jnp.minimum(qn, nq - 1), jn)
    accumulate(qi, j, s_sc[slot])
    return qn, jn

  n_off = nq * (nq - 1) // 2
  if n_off:
    s_sc[0] = scores(1, 0)

    def body(_, carry):
      qi, j = carry
      for u in range(FLASH_UNROLL):
        qi, j = sweep(qi, j, u % 2)
      return qi, j

    qi, j = lax.fori_loop(0, n_off // FLASH_UNROLL, body, (jnp.int32(1), jnp.int32(0)))
    rest = n_off % FLASH_UNROLL
    for u in range(rest):
      qi, j = sweep(qi, j, u % 2, prefetch=u + 1 < rest)

  def finish(d, s):
    accumulate(d, d, jnp.where(col <= row, s, NEG_BIG))
    acc = acc_sc[d]
    acc_sc[d] = acc / acc[:, MLA_V:MLA_V + 1]

  s_sc[0] = scores(0, 0)

  def diag_body(i, carry):
    d = 2 * i
    s_sc[1] = scores(d + 1, d + 1)
    finish(d, s_sc[0])
    nxt = jnp.minimum(d + 2, nq - 1)
    s_sc[0] = scores(nxt, nxt)
    finish(d + 1, s_sc[1])
    return carry

  lax.fori_loop(0, nq // 2, diag_body, 0)
  if nq % 2:
    finish(nq - 1, s_sc[0])

  def write_low(d, carry):
    o_ref[rows_of(d), :MLA_V] = acc_sc[d][:, :MLA_V].astype(BF16)
    return carry

  def write_high(d, carry):
    o_ref[rows_of(d), MLA_V:] = pltpu.roll(acc_sc[d], MLA_V, 1)[:, MLA_V:].astype(BF16)
    return carry

  odd = pl.program_id(1) % 2

  @pl.when(odd == 0)
  def _():
    lax.fori_loop(0, nq, write_low, 0)

  @pl.when(odd == 1)
  def _():
    lax.fori_loop(0, nq, write_high, 0)


def _flash(q, k, v, batch, seq):
  tile = min(FLASH_TILE, seq)
  nq = seq // tile
  per_head = pl.BlockSpec((seq, LANES), lambda b, h: (b, h))
  return pl.pallas_call(
      functools.partial(_flash_kernel, tile=tile, nq=nq),
      grid=(batch, MLA_HEADS),
      in_specs=[per_head, per_head, per_head],
      out_specs=pl.BlockSpec((seq, LANES), lambda b, h: (b, h // 2)),
      out_shape=jax.ShapeDtypeStruct((batch * seq, MLA_HEADS * MLA_V), BF16),
      scratch_shapes=[pltpu.VMEM((2, tile, tile), F32), pltpu.VMEM((nq, tile, LANES), F32)],
      compiler_params=_cparams("parallel", "arbitrary"),
      name="flash",
  )(q, k, v)


def _unit_lower_inverses(l_abs, row, col):
  eye = (row == col).astype(F32)
  same16 = (row // 16) == (col // 16)
  same32 = (row // 32) == (col // 32)
  off16 = same32 & jnp.logical_not(same16)
  x1 = [jnp.where(same16, l, 0.0).astype(BF16) for l in l_abs]
  x2 = [_dot(x, x).astype(BF16) for x in x1]
  x4 = [_dot(x, x).astype(BF16) for x in x2]
  x8 = [_dot(x, x).astype(BF16) for x in x4]
  ts = [eye + x.astype(F32) for x in x1]
  for xs in (x2, x4, x8):
    ts = [t + _dot(t, x) for t, x in zip(ts, xs)]
  for keep in (off16, jnp.logical_not(same32)):
    los = [jnp.where(keep, l, 0.0).astype(BF16) for l in l_abs]
    tbs = [t.astype(BF16) for t in ts]
    mids = [_dot(lo, tb) for lo, tb in zip(los, tbs)]
    ts = [t + _dot(tb, mid) for t, tb, mid in zip(ts, tbs, mids)]
  return ts


def _rwkv_kernel(p_ref, mu_ref, w0_ref, w2_ref, a0_ref, a2_ref, g2_ref, kk_ref, ka_ref, rk_ref,
                 lng_ref, lnb_ref, seg_ref, cum_ref, all_ref, y_ref, prev_sc, state_sc, y_sc, *, tm, nb):
  rows = nb * tm

  @pl.when(pl.program_id(0) == 0)
  def _():
    prev_sc[...] = jnp.zeros_like(prev_sc)
    state_sc[...] = jnp.zeros_like(state_sc)

  p = p_ref[...].reshape(rows, RW_IN)
  rowid = lax.broadcasted_iota(jnp.int32, (rows, 1), 0)
  prev = pltpu.roll(p, 1, 0)
  for b in range(nb):
    prev = jnp.where(rowid == b * tm, prev_sc[8 * b + 7:8 * b + 8, :], prev)
  for b in range(nb):
    prev_sc[8 * b:8 * (b + 1), :] = p[(b + 1) * tm - 8:(b + 1) * tm, :]
  xs = p + (prev - p) * mu_ref[...]
  r = xs[:, :RW_DIM]
  k = xs[:, RW_DIM:2 * RW_DIM]
  v = xs[:, 2 * RW_DIM:3 * RW_DIM]
  lora_in = xs[:, 3 * RW_DIM:3 * RW_DIM + LANES]
  xg = xs[:, 3 * RW_DIM + LANES:]

  w_log = -_softplus(-(w0_ref[...] + _dot(jnp.tanh(lora_in), w2_ref[...]))) - 0.5
  logw = -jnp.exp(w_log)
  a = _sigmoid(a0_ref[...] + _dot(lora_in, a2_ref[...]))
  gate = _dot(_sigmoid(xg), g2_ref[...])
  seg = seg_ref[...]
  kk = k * kk_ref[...]
  kk = kk / jnp.maximum(jnp.sqrt(_dot_exact_rhs(kk * kk, seg)), 1e-12)
  km = k * (1.0 + (a - 1.0) * ka_ref[...])
  b_s = kk * a

  g_in = _dot_exact_lhs(cum_ref[...], logw)
  g_end = _dot_exact_lhs(all_ref[...], logw)
  e_in = jnp.exp(g_in)
  e_neg = jnp.exp(-g_in)
  e_end = jnp.exp(g_end - g_in)
  at = -kk * jnp.exp(g_in - logw)
  bt = b_s * e_neg
  kt = km * e_neg
  rt = r * e_in
  bh = b_s * e_end
  kh = km * e_end
  e_tot = jnp.exp(g_end)

  lane = lax.broadcasted_iota(jnp.int32, (1, LANES), 1)
  first = lane < RW_HEAD
  row = lax.broadcasted_iota(jnp.int32, (LANES, LANES), 0)
  col = lax.broadcasted_iota(jnp.int32, (LANES, LANES), 1)
  same_head = (row // RW_CHUNK) == (col // RW_CHUNK)
  strict = same_head & (col < row)
  incl = same_head & (col <= row)

  def stack(x):
    return jnp.concatenate([jnp.where(first, x, 0.0), jnp.where(first, 0.0, x)], axis=0)

  def dup(x):
    return jnp.concatenate([x, x], axis=0)

  n_pairs = RW_HEADS // 2
  units = [(c, b, pr) for c in range(tm // RW_CHUNK) for b in range(nb) for pr in range(n_pairs)]

  def tile_of(x, c, b, pr):
    r0 = b * tm + RW_CHUNK * c
    return x[r0:r0 + RW_CHUNK, LANES * pr:LANES * (pr + 1)]

  sa = [stack(tile_of(at, *u)).astype(BF16) for u in units]
  sr = [stack(tile_of(rt, *u)).astype(BF16) for u in units]
  sv = [stack(tile_of(v, *u)).astype(BF16) for u in units]
  sc = [_dot_nt(jnp.concatenate([a_, r_], axis=0),
                jnp.concatenate([dup(tile_of(bt, *u)), dup(tile_of(kt, *u))], axis=0))
        for a_, r_, u in zip(sa, sr, units)]
  l_ab = [jnp.where(strict, x[:LANES, :LANES], 0.0) for x in sc]
  l_ak = [jnp.where(strict, x[:LANES, LANES:], 0.0).astype(BF16) for x in sc]
  l_rb = [jnp.where(incl, x[LANES:, :LANES], 0.0).astype(BF16) for x in sc]
  l_rk = [jnp.where(incl, x[LANES:, LANES:], 0.0).astype(BF16) for x in sc]
  bk_t = [jnp.concatenate([stack(tile_of(bh, *u)), stack(tile_of(kh, *u))], axis=0).T.astype(BF16)
          for u in units]
  dec = [jnp.broadcast_to(tile_of(e_tot, *u)[0:1, :], (LANES, LANES)).T for u in units]
  tinv = [t.astype(BF16) for t in _unit_lower_inverses(l_ab, row, col)]
  w1r = [jnp.concatenate([_dot(t, a_).astype(BF16), r_], axis=0) for t, a_, r_ in zip(tinv, sa, sr)]
  lakv = [_dot(l, v_) for l, v_ in zip(l_ak, sv)]
  w2 = [_dot(t, x) for t, x in zip(tinv, lakv)]
  yv = [_dot(l, v_) for l, v_ in zip(l_rk, sv)]

  n_states = nb * n_pairs
  states = [state_sc[i] for i in range(n_states)]
  for c in range(tm // RW_CHUNK):
    ids = range(c * n_states, (c + 1) * n_states)
    both = [_dot(w1r[i], states[i % n_states]) for i in ids]
    us = [bo[:LANES] + w2[i] for bo, i in zip(both, ids)]
    ubs = [u.astype(BF16) for u in us]
    ys = [bo[LANES:] + _dot(l_rb[i], ub) + yv[i] for bo, ub, i in zip(both, ubs, ids)]
    upds = [_dot(bk_t[i], jnp.concatenate([ub, sv[i]], axis=0)) for ub, i in zip(ubs, ids)]
    for y, upd, i in zip(ys, upds, ids):
      _, b, pr = units[i]
      r0 = b * tm + RW_CHUNK * c
      y_sc[r0:r0 + RW_CHUNK, LANES * pr:LANES * (pr + 1)] = y[:RW_CHUNK] + y[RW_CHUNK:]
      states[i % n_states] = states[i % n_states] * dec[i] + upd
  for i in range(n_states):
    state_sc[i] = states[i]

  y = y_sc[...]
  inv_n = 1.0 / RW_HEAD
  mean = _dot_exact_rhs(y, seg) * inv_n
  d = y - mean
  var = _dot_exact_rhs(d * d, seg) * inv_n
  yn = d * lax.rsqrt(var + RW_LN_EPS) * lng_ref[...] + lnb_ref[...]
  bonus = _dot_exact_rhs(r * km * rk_ref[...], seg) * v
  y_ref[...] = ((yn + bonus) * gate).astype(BF16).reshape(nb, tm, RW_DIM)


def _rwkv(prw, batch, seq, mu, w0, w2, a0, a2, g2, k_k, k_a, r_k, ln_g, ln_b):
  tm = min(RW_TILE, seq)
  rows = batch * tm
  zeros = jnp.zeros((RW_DECAY_LORA, RW_DIM), F32)
  w2e = jnp.concatenate([w2, zeros], axis=0).astype(BF16)
  a2e = jnp.concatenate([zeros, a2], axis=0).astype(BF16)
  idx = jnp.arange(RW_DIM) // RW_HEAD
  seg = (idx[:, None] == idx[None, :]).astype(BF16)
  tok = jnp.arange(rows)
  same_chunk = (tok[:, None] // RW_CHUNK) == (tok[None, :] // RW_CHUNK)
  cum = (same_chunk & (tok[None, :] <= tok[:, None])).astype(BF16)
  allc = same_chunk.astype(BF16)
  row = lambda x: x.reshape(1, -1)
  args = (prw.reshape(batch, seq, RW_IN), row(mu), row(w0), w2e, row(a0), a2e, g2.astype(BF16),
          row(k_k), row(k_a), row(r_k), row(ln_g), row(ln_b), seg, cum, allc)
  in_specs = [pl.BlockSpec((batch, tm, RW_IN), lambda i: (0, i, 0))]
  in_specs += [_resident(a) for a in args[1:]]
  out = pl.pallas_call(
      functools.partial(_rwkv_kernel, tm=tm, nb=batch),
      grid=(seq // tm,),
      in_specs=in_specs,
      out_specs=pl.BlockSpec((batch, tm, RW_DIM), lambda i: (0, i, 0)),
      out_shape=jax.ShapeDtypeStruct((batch, seq, RW_DIM), BF16),
      scratch_shapes=[pltpu.VMEM((8 * batch, RW_IN), F32),
                      pltpu.VMEM((batch * (RW_HEADS // 2), LANES, LANES), F32),
                      pltpu.VMEM((rows, RW_DIM), F32)],
      compiler_params=_cparams("arbitrary"),
      name="rwkv",
  )(*args)
  return out.reshape(batch * seq, RW_DIM)


def _memkv_kernel(mem_ref, g_ref, w_ref, kg_ref, k_ref, v_ref):
  xn = _rms(mem_ref[...], g_ref[...]).astype(BF16)
  kv = jnp.dot(xn, w_ref[...], preferred_element_type=F32)
  for hh in range(X_HEADS):
    sl = slice(X_HEAD * hh, X_HEAD * (hh + 1))
    k_ref[:, sl] = _rms(kv[:, sl], kg_ref[...]).astype(BF16)
  v_ref[...] = kv[:, X_DIM:].astype(BF16)


def _memkv(mem2, batch, g, wkv, k_hnorm):
  args = (mem2, g.reshape(1, -1), wkv.astype(BF16), k_hnorm.reshape(1, -1))
  return pl.pallas_call(
      _memkv_kernel,
      grid=(batch,),
      in_specs=[_rows(MEM_LEN, D_MODEL)] + [_resident(a) for a in args[1:]],
      out_specs=[_rows(MEM_LEN, X_DIM)] * 2,
      out_shape=[jax.ShapeDtypeStruct((batch * MEM_LEN, X_DIM), BF16)] * 2,
      compiler_params=_cparams("parallel"),
      name="memkv",
  )(*args)


def _xattn_block(h, k_ref, v_ref, g_ref, wq_ref, qg_ref, wo_ref):
  xn = _rms(h, g_ref[...]).astype(BF16)
  q = jnp.dot(xn, wq_ref[...], preferred_element_type=F32)
  scale = X_HEAD ** -0.5
  outs = []
  for hh in range(X_HEADS):
    sl = slice(X_HEAD * hh, X_HEAD * (hh + 1))
    qh = _rms(q[:, sl], qg_ref[...]) * scale
    s = _dot_nt(qh, k_ref[:, sl])
    e = jnp.exp(s - jnp.max(s, axis=-1, keepdims=True))
    p = e / jnp.sum(e, axis=-1, keepdims=True)
    outs.append(jnp.dot(p.astype(BF16), v_ref[:, sl], preferred_element_type=F32))
  att = jnp.concatenate(outs, axis=-1).astype(BF16)
  return h + jnp.dot(att, wo_ref[...], preferred_element_type=F32)


FFN_CHUNK = 256


def _ffn_block(h, g_ref, w1_ref, w3_ref, w2_ref):
  xn = _rms(h, g_ref[...]).astype(BF16)
  acc = h
  for c in range(FFN_HIDDEN // FFN_CHUNK):
    sl = slice(FFN_CHUNK * c, FFN_CHUNK * (c + 1))
    u1 = jnp.dot(xn, w1_ref[:, sl], preferred_element_type=F32)
    u3 = jnp.dot(xn, w3_ref[:, sl], preferred_element_type=F32)
    act = (_silu(u1) * u3).astype(BF16)
    acc = acc + jnp.dot(act, w2_ref[sl, :], preferred_element_type=F32)
  return acc


def _tail_kernel(*refs, n):
  h_ref, y_refs, (k_ref, v_ref), w_refs = refs[0], refs[1:1 + n], refs[1 + n:3 + n], refs[3 + n:3 + 2 * n]
  xa_g, wq, qg, wo, ffn_g, w1, w3, w2, o_ref = refs[3 + 2 * n:]
  h = h_ref[...]
  for y_ref, w_ref in zip(y_refs, w_refs):
    h = h + jnp.dot(y_ref[...], w_ref[...], preferred_element_type=F32)
  h = _xattn_block(h, k_ref, v_ref, xa_g, wq, qg, wo)
  o_ref[...] = _ffn_block(h, ffn_g, w1, w3, w2)


def _tail(h, ys, ws, kmem, vmem, seq, xa_g, xa_wq, xa_q_hnorm, xa_wo, ffn_g, ffn_w13, ffn_w2):
  t = h.shape[0]
  tm = min(TOKEN_TILE, seq)
  per_batch = seq // tm
  n = len(ys)
  consts = (xa_g.reshape(1, -1), xa_wq.astype(BF16), xa_q_hnorm.reshape(1, -1), xa_wo.astype(BF16),
            ffn_g.reshape(1, -1), ffn_w13[:, :FFN_HIDDEN].astype(BF16),
            ffn_w13[:, FFN_HIDDEN:].astype(BF16), ffn_w2.astype(BF16))
  mem_spec = pl.BlockSpec((MEM_LEN, X_DIM), lambda i: (i // per_batch, 0))
  in_specs = ([_rows(tm, D_MODEL)] + [_rows(tm, y.shape[1]) for y in ys] + [mem_spec, mem_spec] +
              [_resident(a) for a in (*ws, *consts)])
  return pl.pallas_call(
      functools.partial(_tail_kernel, n=n),
      grid=(t // tm,),
      in_specs=in_specs,
      out_specs=_rows(tm, D_MODEL),
      out_shape=jax.ShapeDtypeStruct((t, D_MODEL), F32),
      compiler_params=_cparams("parallel"),
      name="tail",
  )(h, *ys, kmem, vmem, *ws, *consts)


def _odd_in_kernel(h_ref, g_ref, wz_ref, wx_ref, wdt_ref, cw_ref, cb_ref, z_ref, xbc_ref, dt_ref,
                   xbuf_sc, *, tm, tiles_per_seq):
  @pl.when(pl.program_id(0) % tiles_per_seq == 0)
  def _():
    xbuf_sc[0:8, :] = jnp.zeros((8, SSM_CONV_DIM), F32)

  xn = _rms(h_ref[...], g_ref[...]).astype(BF16)
  dt_ref[...] = jnp.dot(xn, wdt_ref[...], preferred_element_type=F32)
  n_blocks = SSM_CONV_DIM // CONV_COLS

  def project(c):
    cs = slice(CONV_COLS * c, CONV_COLS * (c + 1))
    xbuf_sc[8:8 + tm, cs] = jnp.dot(xn, wx_ref[:, cs], preferred_element_type=F32)

  project(0)
  for c in range(n_blocks):
    cs = slice(CONV_COLS * c, CONV_COLS * (c + 1))
    if c + 1 < n_blocks:
      project(c + 1)
    xe = xbuf_sc[:, cs]
    acc = cw_ref[0:1, cs] * xe
    for j in range(1, SSM_CONV):
      acc = cw_ref[j:j + 1, cs] * xe + pltpu.roll(acc, 1, 0)
    xbuf_sc[0:8, cs] = xe[tm:tm + 8, :]
    xbc_ref[:, cs] = _silu(acc[8:, :] + cb_ref[:, cs]).astype(BF16)
    zs = slice(CONV_COLS * c, min(CONV_COLS * (c + 1), SSM_INNER))
    if zs.start < SSM_INNER:
      z_ref[:, zs] = jnp.dot(xn, wz_ref[:, zs], preferred_element_type=F32).astype(BF16)


def _odd_in(h, seq, g, w_in, conv_w, conv_b):
  t = h.shape[0]
  tm = min(TOKEN_TILE, seq)
  wz = w_in[:, :SSM_INNER].astype(BF16)
  wx = w_in[:, SSM_INNER:SSM_INNER + SSM_CONV_DIM].astype(BF16)
  wdt = w_in[:, SSM_INNER + SSM_CONV_DIM:]
  wdt = jnp.concatenate([wdt, jnp.zeros((D_MODEL, LANES - SSM_HEADS), F32)], axis=1).astype(BF16)
  args = (h, g.reshape(1, -1), wz, wx, wdt, conv_w, conv_b.reshape(1, -1))
  return pl.pallas_call(
      functools.partial(_odd_in_kernel, tm=tm, tiles_per_seq=seq // tm),
      grid=(t // tm,),
      in_specs=[_rows(tm, D_MODEL)] + [_resident(a) for a in args[1:]],
      out_specs=[_rows(tm, SSM_INNER), _rows(tm, SSM_CONV_DIM), _rows(tm, LANES)],
      out_shape=[jax.ShapeDtypeStruct((t, SSM_INNER), BF16),
                 jax.ShapeDtypeStruct((t, SSM_CONV_DIM), BF16),
                 jax.ShapeDtypeStruct((t, LANES), F32)],
      scratch_shapes=[pltpu.VMEM((tm + 8, SSM_CONV_DIM), F32)],
      compiler_params=_cparams("arbitrary"),
      name="odd_in",
  )(*args)


def _ssd_kernel(xbc_ref, z_ref, dt_ref, dtb_ref, alog_ref, dskip_ref, gn_ref, exp_ref, tri_ref,
                y_ref, state_sc, yacc_sc, *, L):
  @pl.when(pl.program_id(1) == 0)
  def _():
    state_sc[...] = jnp.zeros_like(state_sc)

  gn_w = SSM_GROUPS * SSM_STATE
  x = xbc_ref[:, :SSM_INNER]
  b_all = xbc_ref[:, SSM_INNER:SSM_INNER + gn_w]
  c_all = xbc_ref[:, SSM_INNER + gn_w:]

  dt = _softplus(dt_ref[...] + dtb_ref[...])
  a2 = dt * (-jnp.exp(alog_ref[...]) * LOG2E)
  cum = _dot_exact_lhs(tri_ref[...], a2)
  last = cum[L - 1:L, :]
  key_t = (cum - jnp.log2(dt)).T
  expand = exp_ref[...]
  ecum_x = _dot(jnp.exp2(cum), expand)
  wend_x = _dot(dt * jnp.exp2(last - cum), expand)
  elast_x = _dot_exact_rhs(jnp.broadcast_to(jnp.exp2(last), (8, LANES)), expand)[0:1, :]
  xend = (x * wend_x).astype(BF16)

  row = lax.broadcasted_iota(jnp.int32, (L, L), 0)
  col = lax.broadcasted_iota(jnp.int32, (L, L), 1)
  causal = col <= row
  lane = lax.broadcasted_iota(jnp.int32, (1, LANES), 1)
  first = lane < SSM_HEAD
  zero = jnp.zeros((), BF16)

  e_per = SSM_HEADS // SSM_GROUPS
  for g in range(SSM_GROUPS):
    gs = slice(SSM_GROUP_W * g, SSM_GROUP_W * (g + 1))
    bg = b_all[:, SSM_STATE * g:SSM_STATE * (g + 1)]
    cg = c_all[:, SSM_STATE * g:SSM_STATE * (g + 1)]
    cb = _dot_nt(cg, bg)
    st = state_sc[:, gs]
    yacc_sc[:, gs] = _dot(cg, st) * ecum_x[:, gs]
    for pr in range(e_per // 2):
      ps = slice(SSM_GROUP_W * g + LANES * pr, SSM_GROUP_W * g + LANES * (pr + 1))
      xp = x[:, ps]
      y_pair = jnp.zeros((L, LANES), F32)
      for j in range(2):
        hd = e_per * g + 2 * pr + j
        decay = jnp.exp2(cum[:, hd:hd + 1] - key_t[hd:hd + 1, :])
        m = jnp.where(causal, cb * decay, 0.0).astype(BF16)
        xj = jnp.where(first, xp, zero) if j == 0 else jnp.where(first, zero, xp)
        y_pair = y_pair + jnp.dot(m, xj, preferred_element_type=F32)
      yacc_sc[:, ps] += y_pair
    state_sc[:, gs] = st * elast_x[:, gs] + _dot_tn(bg, xend[:, gs])

  y = yacc_sc[...] + x * dskip_ref[...]
  y = y * _silu(z_ref[...].astype(F32))
  for g in range(SSM_GROUPS):
    gs = slice(SSM_GROUP_W * g, SSM_GROUP_W * (g + 1))
    y_ref[:, gs] = _rms(y[:, gs], gn_ref[:, gs]).astype(BF16)


def _ssd(z, xbc, dt, batch, seq, dt_bias, a_log, d_skip, gnorm):
  L = min(SSM_CHUNK, seq)
  nc = seq // L
  pad = lambda x: jnp.concatenate([x, jnp.zeros((LANES - SSM_HEADS,), F32)]).reshape(1, LANES)
  head_of_lane = jnp.arange(SSM_INNER) // SSM_HEAD
  expand = (jnp.arange(LANES)[:, None] == head_of_lane[None, :]).astype(BF16)
  tok = jnp.arange(L)
  tri = (tok[None, :] <= tok[:, None]).astype(BF16)
  args = (xbc, z, dt, pad(dt_bias), pad(a_log), jnp.repeat(d_skip, SSM_HEAD).reshape(1, -1),
          gnorm.reshape(1, -1), expand, tri)
  tiles = lambda w: pl.BlockSpec((L, w), lambda b, i: (b * nc + i, 0))
  return pl.pallas_call(
      functools.partial(_ssd_kernel, L=L),
      grid=(batch, nc),
      in_specs=[tiles(SSM_CONV_DIM), tiles(SSM_INNER), tiles(LANES)] + [_resident(a) for a in args[3:]],
      out_specs=tiles(SSM_INNER),
      out_shape=jax.ShapeDtypeStruct((batch * seq, SSM_INNER), BF16),
      scratch_shapes=[pltpu.VMEM((SSM_STATE, SSM_INNER), F32),
                      pltpu.VMEM((L, SSM_INNER), F32)],
      compiler_params=_cparams("parallel", "arbitrary"),
      name="ssd",
  )(*args)


def kernel(x, mem, positions, ev_norm, ev_w_in, mla_q_norm, mla_w_uq, mla_kv_norm, mla_w_ukv, mla_q_hnorm, mla_k_hnorm, rw_mu, rw_w0, rw_w2, rw_a0, rw_a2, rw_g2, rw_k_k, rw_k_a, rw_r_k, rw_ln_g, rw_ln_b, ev_w_out, od_norm, od_w_in, ssm_conv_w, ssm_conv_b, ssm_dt_bias, ssm_a_log, ssm_d, ssm_gnorm, od_w_out, xa_norm_x, xa_norm_mem, xa_wq, xa_wkv, xa_q_hnorm, xa_k_hnorm, xa_wo, ffn_norm, ffn_w13, ffn_w2):
  batch, seq, _ = x.shape
  depth = xa_wq.shape[0]
  h = x.reshape(batch * seq, D_MODEL)
  mem2 = mem.reshape(batch * MEM_LEN, D_MODEL)
  pos = positions.reshape(batch * seq)
  for i in range(depth):
    j = i // 2
    kmem, vmem = _memkv(mem2, batch, xa_norm_mem[i], xa_wkv[i], xa_k_hnorm[i])
    if i % 2 == 0:
      q, k, v, prw = _even_in(h, pos, ev_norm[j], ev_w_in[j], mla_q_norm[j], mla_w_uq[j],
                              mla_kv_norm[j], mla_w_ukv[j], mla_q_hnorm[j], mla_k_hnorm[j])
      y_mla = _flash(q, k, v, batch, seq)
      y_rw = _rwkv(prw, batch, seq, rw_mu[j], rw_w0[j], rw_w2[j], rw_a0[j], rw_a2[j], rw_g2[j],
                   rw_k_k[j], rw_k_a[j], rw_r_k[j].reshape(-1), rw_ln_g[j], rw_ln_b[j])
      w_out = ev_w_out[j].astype(BF16)
      ys, ws = [y_mla, y_rw], [w_out[:MLA_HEADS * MLA_V], w_out[MLA_HEADS * MLA_V:]]
    else:
      z, xbc, dt = _odd_in(h, seq, od_norm[j], od_w_in[j], ssm_conv_w[j], ssm_conv_b[j])
      y = _ssd(z, xbc, dt, batch, seq, ssm_dt_bias[j], ssm_a_log[j], ssm_d[j], ssm_gnorm[j])
      ys, ws = [y], [od_w_out[j].astype(BF16)]
    h = _tail(h, ys, ws, kmem, vmem, seq, xa_norm_x[i], xa_wq[i], xa_q_hnorm[i], xa_wo[i],
              ffn_norm[i], ffn_w13[i], ffn_w2[i])
  return h.reshape(batch, seq, D_MODEL)
```

```python
import functools

import jax
import jax.numpy as jnp
from jax import lax
from jax.experimental import pallas as pl
from jax.experimental.pallas import tpu as pltpu

F32 = jnp.float32
BF16 = jnp.bfloat16

D_MODEL = 1024
MEM_LEN = 256
NORM_EPS = 1e-6

MLA_HEADS = 8
MLA_NOPE = 64
MLA_ROPE = 32
MLA_QK = MLA_NOPE + MLA_ROPE
MLA_V = 64
MLA_Q_RANK = 384
MLA_KV_RANK = 256
ROPE_BASE = 10000.0
MLA_IN = MLA_Q_RANK + MLA_KV_RANK + MLA_ROPE

RW_HEADS = 8
RW_HEAD = 64
RW_DIM = RW_HEADS * RW_HEAD
RW_DECAY_LORA = 64
RW_AAA_LORA = 64
RW_GATE_LORA = 128
RW_LN_EPS = 64e-5
RW_IN = 3 * RW_DIM + RW_DECAY_LORA + RW_AAA_LORA + RW_GATE_LORA
RW_CHUNK = 64

SSM_INNER = 2 * D_MODEL
SSM_HEAD = 64
SSM_HEADS = SSM_INNER // SSM_HEAD
SSM_GROUPS = 4
SSM_STATE = 128
SSM_CONV = 4
SSM_CHUNK = 256
SSM_CONV_DIM = SSM_INNER + 2 * SSM_GROUPS * SSM_STATE
SSM_GROUP_W = SSM_INNER // SSM_GROUPS

X_HEADS = 4
X_HEAD = 128
X_DIM = X_HEADS * X_HEAD

FFN_HIDDEN = -((-8 * D_MODEL) // (3 * 256)) * 256

LANES = 128
V7X_VMEM_BYTES = 64 * 1024 * 1024
VMEM_COMPILER_RESERVE = 8 * 1024 * 1024
VMEM_LIMIT = V7X_VMEM_BYTES - VMEM_COMPILER_RESERVE

TOKEN_TILE = 512
RW_TILE = 128
FLASH_TILE = 1024
FLASH_UNROLL = 8
CONV_COLS = 512
NEG_BIG = -1e30
LOG2E = 1.4426950408889634


def _cparams(*sem):
  return pltpu.CompilerParams(dimension_semantics=sem, vmem_limit_bytes=VMEM_LIMIT)


def _resident(arr):
  nd = arr.ndim
  return pl.BlockSpec(arr.shape, lambda *_: (0,) * nd, pipeline_mode=pl.Buffered(1))


def _rows(tm, width):
  return pl.BlockSpec((tm, width), lambda i: (i, 0))


def _rms(x, g, eps=NORM_EPS):
  return x * lax.rsqrt(jnp.mean(x * x, axis=-1, keepdims=True) + eps) * g


def _dot(a, b):
  return jnp.dot(a.astype(BF16), b.astype(BF16), preferred_element_type=F32)


def _dot_nt(a, b):
  return lax.dot_general(a.astype(BF16), b.astype(BF16), (((1,), (1,)), ((), ())),
                         preferred_element_type=F32)


def _dot_tn(a, b):
  return lax.dot_general(a.astype(BF16), b.astype(BF16), (((0,), (0,)), ((), ())),
                         preferred_element_type=F32)


def _split(x):
  hi = x.astype(BF16)
  lo = (x - hi.astype(F32)).astype(BF16)
  return hi, lo


def _dot_exact_lhs(m, x):
  hi, lo = _split(x)
  return (jnp.dot(m, hi, preferred_element_type=F32) +
          jnp.dot(m, lo, preferred_element_type=F32))


def _dot_exact_rhs(x, m):
  hi, lo = _split(x)
  return (jnp.dot(hi, m, preferred_element_type=F32) +
          jnp.dot(lo, m, preferred_element_type=F32))


def _sigmoid(x):
  return 1.0 / (1.0 + jnp.exp2(x * (-LOG2E)))


def _silu(x):
  return x * _sigmoid(x)


def _softplus(x):
  return jnp.maximum(x, 0.0) + jnp.log(1.0 + jnp.exp(-jnp.abs(x)))


def _even_in_kernel(h_ref, pos_ref, g_ref, wa_ref, wrw_ref, qn_ref, wq_ref, kvn_ref, wk_ref,
                    wv_ref, gq_ref, gk_ref, invf_ref, q_ref, k_ref, v_ref, prw_ref):
  xn = _rms(h_ref[...], g_ref[...]).astype(BF16)
  pa = jnp.dot(xn, wa_ref[...], preferred_element_type=F32)
  cq = _rms(pa[:, :MLA_Q_RANK], qn_ref[...]).astype(BF16)
  ckv = _rms(pa[:, MLA_Q_RANK:MLA_Q_RANK + MLA_KV_RANK], kvn_ref[...]).astype(BF16)
  kr = pa[:, MLA_Q_RANK + MLA_KV_RANK:]

  lane = lax.broadcasted_iota(jnp.int32, (1, LANES), 1)
  ang = pos_ref[...].astype(F32) * invf_ref[...]
  cs = jnp.where(lane < MLA_NOPE, 1.0, jnp.where(lane < MLA_QK, jnp.cos(ang), jnp.sin(ang)))
  in_head = lane < MLA_QK

  zq = jnp.dot(cq, wq_ref[...], preferred_element_type=F32)
  zk = jnp.dot(ckv, wk_ref[...], preferred_element_type=F32)
  zv = jnp.dot(ckv, wv_ref[...], preferred_element_type=F32)

  r_id = lax.broadcasted_iota(jnp.int32, (LANES, LANES), 0)
  c_id = lax.broadcasted_iota(jnp.int32, (LANES, LANES), 1)
  target = jnp.where(r_id < MLA_QK, r_id, r_id - MLA_ROPE)
  fold = jnp.where(target == c_id, 1.0, 0.0).astype(BF16)
  q_mult = gq_ref[...] * cs * (MLA_QK ** -0.5 * LOG2E)
  k_mult = gk_ref[...] * cs

  def head_norm_rope(z, mult):
    ms = jnp.sum(jnp.where(in_head, z * z, 0.0), axis=-1, keepdims=True) * (1.0 / MLA_QK)
    zn = (z * lax.rsqrt(ms + NORM_EPS) * mult).astype(BF16)
    return jnp.dot(zn, fold, preferred_element_type=F32).astype(BF16)

  rw_cols = 2 * LANES
  for hh in range(MLA_HEADS):
    sl = slice(LANES * hh, LANES * (hh + 1))
    q_ref[:, sl] = head_norm_rope(zq[:, sl], q_mult)
    k_ref[:, sl] = head_norm_rope(zk[:, sl] + kr, k_mult)
    v_ref[:, sl] = jnp.where(lane == MLA_V, 1.0, zv[:, sl]).astype(BF16)
    if hh < RW_IN // rw_cols:
      rws = slice(rw_cols * hh, rw_cols * (hh + 1))
      prw_ref[:, rws] = jnp.dot(xn, wrw_ref[:, rws], preferred_element_type=F32)


def _rot_cols(w):
  half = MLA_ROPE // 2
  return jnp.concatenate([-w[..., half:], w[..., :half]], axis=-1)


def _swap_halves(g):
  half = MLA_ROPE // 2
  return jnp.concatenate([g[..., half:], g[..., :half]], axis=-1)


def _even_in(h, pos, norm, w_in, q_norm, w_uq, kv_norm, w_ukv, q_hnorm, k_hnorm):
  t = h.shape[0]
  tm = min(TOKEN_TILE, t)
  w_cq = w_in[:, :MLA_Q_RANK]
  w_ckv = w_in[:, MLA_Q_RANK:MLA_Q_RANK + MLA_KV_RANK]
  w_kr = w_in[:, MLA_Q_RANK + MLA_KV_RANK:MLA_IN]
  w_kr_ext = jnp.concatenate([jnp.zeros((D_MODEL, MLA_NOPE), F32), w_kr, _rot_cols(w_kr)], axis=1)
  wa = jnp.concatenate([w_cq, w_ckv, w_kr_ext], axis=1).astype(BF16)
  wrw = w_in[:, MLA_IN:].astype(BF16)

  wq3 = w_uq.reshape(MLA_Q_RANK, MLA_HEADS, MLA_QK)
  wq_ext = jnp.concatenate([wq3, _rot_cols(wq3[..., MLA_NOPE:])], axis=-1)
  wq_ext = wq_ext.reshape(MLA_Q_RANK, MLA_HEADS * LANES).astype(BF16)
  wkv3 = w_ukv.reshape(MLA_KV_RANK, MLA_HEADS, MLA_NOPE + MLA_V)
  pad = jnp.zeros((MLA_KV_RANK, MLA_HEADS, LANES - MLA_NOPE), F32)
  wk_ext = jnp.concatenate([wkv3[..., :MLA_NOPE], pad], axis=-1)
  wk_ext = wk_ext.reshape(MLA_KV_RANK, MLA_HEADS * LANES).astype(BF16)
  wv_ext = jnp.concatenate([wkv3[..., MLA_NOPE:], pad], axis=-1)
  wv_ext = wv_ext.reshape(MLA_KV_RANK, MLA_HEADS * LANES).astype(BF16)

  def g_ext(g):
    return jnp.concatenate([g, _swap_halves(g[MLA_NOPE:])]).reshape(1, LANES)

  half = MLA_ROPE // 2
  inv_freq = ROPE_BASE ** (-jnp.arange(half, dtype=F32) / half)
  invf = jnp.concatenate([jnp.zeros((MLA_NOPE,), F32)] + [inv_freq] * 4).reshape(1, LANES)

  args = (h, pos.reshape(t, 1), norm.reshape(1, -1), wa, wrw, q_norm.reshape(1, -1), wq_ext,
          kv_norm.reshape(1, -1), wk_ext, wv_ext, g_ext(q_hnorm), g_ext(k_hnorm), invf)
  in_specs = [_rows(tm, D_MODEL), _rows(tm, 1)] + [_resident(a) for a in args[2:]]
  hw = MLA_HEADS * LANES
  return pl.pallas_call(
      _even_in_kernel,
      grid=(t // tm,),
      in_specs=in_specs,
      out_specs=[_rows(tm, hw), _rows(tm, hw), _rows(tm, hw), _rows(tm, RW_IN)],
      out_shape=[jax.ShapeDtypeStruct((t, hw), BF16)] * 3 + [jax.ShapeDtypeStruct((t, RW_IN), F32)],
      compiler_params=_cparams("parallel"),
      name="even_in",
  )(*args)


M_LANE = MLA_V + 1


def _flash_kernel(q_ref, k_ref, v_ref, o_ref, s_sc, acc_sc, *, tile, nq):
  lane = lax.broadcasted_iota(jnp.int32, (1, LANES), 1)
  row = lax.broadcasted_iota(jnp.int32, (tile, tile), 0)
  col = lax.broadcasted_iota(jnp.int32, (tile, tile), 1)
  acc_sc[...] = jnp.broadcast_to(jnp.where(lane == M_LANE, NEG_BIG, 0.0), acc_sc.shape)

  def rows_of(i):
    return pl.ds(pl.multiple_of(i * tile, tile), tile)

  def scores(qi, j):
    return _dot_nt(q_ref[rows_of(qi), :], k_ref[rows_of(j), :])

  def accumulate(qi, j, s):
    acc = acc_sc[qi]
    m = acc[:, M_LANE:M_LANE + 1]
    part = s[:, :LANES]
    for c in range(1, tile // LANES):
      part = jnp.maximum(part, s[:, LANES * c:LANES * (c + 1)])
    m_new = jnp.maximum(m, jnp.max(part, axis=-1, keepdims=True))
    p = jnp.exp2(s - m_new).astype(BF16)
    pv = jnp.dot(p, v_ref[rows_of(j), :], preferred_element_type=F32)
    acc_sc[qi] = jnp.where(lane == M_LANE, m_new, acc * jnp.exp2(m - m_new) + pv)

  def advance(qi, j):
    wrap = j + 1 >= qi
    return jnp.where(wrap, qi + 1, qi), jnp.where(wrap, 0, j + 1)

  def sweep(qi, j, slot, prefetch=True):
    qn, jn = advance(qi, j)
    if prefetch:
      s_sc[1 - slot] = scores(jnp.minimum(qn, nq - 1), jn)
    accumulate(qi, j, s_sc[slot])
    return qn, jn

  n_off = nq * (nq - 1) // 2
  if n_off:
    s_sc[0] = scores(1, 0)

    def body(_, carry):
      qi, j = carry
      for u in range(FLASH_UNROLL):
        qi, j = sweep(qi, j, u % 2)
      return qi, j

    qi, j = lax.fori_loop(0, n_off // FLASH_UNROLL, body, (jnp.int32(1), jnp.int32(0)))
    rest = n_off % FLASH_UNROLL
    for u in range(rest):
      qi, j = sweep(qi, j, u % 2, prefetch=u + 1 < rest)

  def finish(d, s):
    accumulate(d, d, jnp.where(col <= row, s, NEG_BIG))
    acc = acc_sc[d]
    acc_sc[d] = acc / acc[:, MLA_V:MLA_V + 1]

  s_sc[0] = scores(0, 0)

  def diag_body(i, carry):
    d = 2 * i
    s_sc[1] = scores(d + 1, d + 1)
    finish(d, s_sc[0])
    nxt = jnp.minimum(d + 2, nq - 1)
    s_sc[0] = scores(nxt, nxt)
    finish(d + 1, s_sc[1])
    return carry

  lax.fori_loop(0, nq // 2, diag_body, 0)
  if nq % 2:
    finish(nq - 1, s_sc[0])

  def write_low(d, carry):
    o_ref[rows_of(d), :MLA_V] = acc_sc[d][:, :MLA_V].astype(BF16)
    return carry

  def write_high(d, carry):
    o_ref[rows_of(d), MLA_V:] = pltpu.roll(acc_sc[d], MLA_V, 1)[:, MLA_V:].astype(BF16)
    return carry

  odd = pl.program_id(1) % 2

  @pl.when(odd == 0)
  def _():
    lax.fori_loop(0, nq, write_low, 0)

  @pl.when(odd == 1)
  def _():
    lax.fori_loop(0, nq, write_high, 0)


def _flash(q, k, v, batch, seq):
  tile = min(FLASH_TILE, seq)
  nq = seq // tile
  per_head = pl.BlockSpec((seq, LANES), lambda b, h: (b, h))
  return pl.pallas_call(
      functools.partial(_flash_kernel, tile=tile, nq=nq),
      grid=(batch, MLA_HEADS),
      in_specs=[per_head, per_head, per_head],
      out_specs=pl.BlockSpec((seq, LANES), lambda b, h: (b, h // 2)),
      out_shape=jax.ShapeDtypeStruct((batch * seq, MLA_HEADS * MLA_V), BF16),
      scratch_shapes=[pltpu.VMEM((2, tile, tile), F32), pltpu.VMEM((nq, tile, LANES), F32)],
      compiler_params=_cparams("parallel", "arbitrary"),
      name="flash",
  )(q, k, v)


def _unit_lower_inverses(l_abs, row, col):
  eye = (row == col).astype(F32)
  same16 = (row // 16) == (col // 16)
  same32 = (row // 32) == (col // 32)
  off16 = same32 & jnp.logical_not(same16)
  x1 = [jnp.where(same16, l, 0.0).astype(BF16) for l in l_abs]
  x2 = [_dot(x, x).astype(BF16) for x in x1]
  x4 = [_dot(x, x).astype(BF16) for x in x2]
  x8 = [_dot(x, x).astype(BF16) for x in x4]
  ts = [eye + x.astype(F32) for x in x1]
  for xs in (x2, x4, x8):
    ts = [t + _dot(t, x) for t, x in zip(ts, xs)]
  for keep in (off16, jnp.logical_not(same32)):
    los = [jnp.where(keep, l, 0.0).astype(BF16) for l in l_abs]
    tbs = [t.astype(BF16) for t in ts]
    mids = [_dot(lo, tb) for lo, tb in zip(los, tbs)]
    ts = [t + _dot(tb, mid) for t, tb, mid in zip(ts, tbs, mids)]
  return ts


def _rwkv_kernel(p_ref, mu_ref, w0_ref, w2_ref, a0_ref, a2_ref, g2_ref, kk_ref, ka_ref, rk_ref,
                 lng_ref, lnb_ref, seg_ref, cum_ref, all_ref, y_ref, prev_sc, state_sc, y_sc, *, tm, nb):
  rows = nb * tm

  @pl.when(pl.program_id(0) == 0)
  def _():
    prev_sc[...] = jnp.zeros_like(prev_sc)
    state_sc[...] = jnp.zeros_like(state_sc)

  p = p_ref[...].reshape(rows, RW_IN)
  rowid = lax.broadcasted_iota(jnp.int32, (rows, 1), 0)
  prev = pltpu.roll(p, 1, 0)
  for b in range(nb):
    prev = jnp.where(rowid == b * tm, prev_sc[8 * b + 7:8 * b + 8, :], prev)
  for b in range(nb):
    prev_sc[8 * b:8 * (b + 1), :] = p[(b + 1) * tm - 8:(b + 1) * tm, :]
  xs = p + (prev - p) * mu_ref[...]
  r = xs[:, :RW_DIM]
  k = xs[:, RW_DIM:2 * RW_DIM]
  v = xs[:, 2 * RW_DIM:3 * RW_DIM]
  lora_in = xs[:, 3 * RW_DIM:3 * RW_DIM + LANES]
  xg = xs[:, 3 * RW_DIM + LANES:]

  w_log = -_softplus(-(w0_ref[...] + _dot(jnp.tanh(lora_in), w2_ref[...]))) - 0.5
  logw = -jnp.exp(w_log)
  a = _sigmoid(a0_ref[...] + _dot(lora_in, a2_ref[...]))
  gate = _dot(_sigmoid(xg), g2_ref[...])
  seg = seg_ref[...]
  kk = k * kk_ref[...]
  kk = kk / jnp.maximum(jnp.sqrt(_dot_exact_rhs(kk * kk, seg)), 1e-12)
  km = k * (1.0 + (a - 1.0) * ka_ref[...])
  b_s = kk * a

  g_in = _dot_exact_lhs(cum_ref[...], logw)
  g_end = _dot_exact_lhs(all_ref[...], logw)
  e_in = jnp.exp(g_in)
  e_neg = jnp.exp(-g_in)
  e_end = jnp.exp(g_end - g_in)
  at = -kk * jnp.exp(g_in - logw)
  bt = b_s * e_neg
  kt = km * e_neg
  rt = r * e_in
  bh = b_s * e_end
  kh = km * e_end
  e_tot = jnp.exp(g_end)

  lane = lax.broadcasted_iota(jnp.int32, (1, LANES), 1)
  first = lane < RW_HEAD
  row = lax.broadcasted_iota(jnp.int32, (LANES, LANES), 0)
  col = lax.broadcasted_iota(jnp.int32, (LANES, LANES), 1)
  same_head = (row // RW_CHUNK) == (col // RW_CHUNK)
  strict = same_head & (col < row)
  incl = same_head & (col <= row)

  def stack(x):
    return jnp.concatenate([jnp.where(first, x, 0.0), jnp.where(first, 0.0, x)], axis=0)

  def dup(x):
    return jnp.concatenate([x, x], axis=0)

  n_pairs = RW_HEADS // 2
  units = [(c, b, pr) for c in range(tm // RW_CHUNK) for b in range(nb) for pr in range(n_pairs)]

  def tile_of(x, c, b, pr):
    r0 = b * tm + RW_CHUNK * c
    return x[r0:r0 + RW_CHUNK, LANES * pr:LANES * (pr + 1)]

  sa = [stack(tile_of(at, *u)).astype(BF16) for u in units]
  sr = [stack(tile_of(rt, *u)).astype(BF16) for u in units]
  sv = [stack(tile_of(v, *u)).astype(BF16) for u in units]
  sc = [_dot_nt(jnp.concatenate([a_, r_], axis=0),
                jnp.concatenate([dup(tile_of(bt, *u)), dup(tile_of(kt, *u))], axis=0))
        for a_, r_, u in zip(sa, sr, units)]
  l_ab = [jnp.where(strict, x[:LANES, :LANES], 0.0) for x in sc]
  l_ak = [jnp.where(strict, x[:LANES, LANES:], 0.0).astype(BF16) for x in sc]
  l_rb = [jnp.where(incl, x[LANES:, :LANES], 0.0).astype(BF16) for x in sc]
  l_rk = [jnp.where(incl, x[LANES:, LANES:], 0.0).astype(BF16) for x in sc]
  bk_t = [jnp.concatenate([stack(tile_of(bh, *u)), stack(tile_of(kh, *u))], axis=0).T.astype(BF16)
          for u in units]
  dec = [jnp.broadcast_to(tile_of(e_tot, *u)[0:1, :], (LANES, LANES)).T for u in units]
  tinv = [t.astype(BF16) for t in _unit_lower_inverses(l_ab, row, col)]
  w1r = [jnp.concatenate([_dot(t, a_).astype(BF16), r_], axis=0) for t, a_, r_ in zip(tinv, sa, sr)]
  lakv = [_dot(l, v_) for l, v_ in zip(l_ak, sv)]
  w2 = [_dot(t, x) for t, x in zip(tinv, lakv)]
  yv = [_dot(l, v_) for l, v_ in zip(l_rk, sv)]

  n_states = nb * n_pairs
  states = [state_sc[i] for i in range(n_states)]
  for c in range(tm // RW_CHUNK):
    ids = range(c * n_states, (c + 1) * n_states)
    both = [_dot(w1r[i], states[i % n_states]) for i in ids]
    us = [bo[:LANES] + w2[i] for bo, i in zip(both, ids)]
    ubs = [u.astype(BF16) for u in us]
    ys = [bo[LANES:] + _dot(l_rb[i], ub) + yv[i] for bo, ub, i in zip(both, ubs, ids)]
    upds = [_dot(bk_t[i], jnp.concatenate([ub, sv[i]], axis=0)) for ub, i in zip(ubs, ids)]
    for y, upd, i in zip(ys, upds, ids):
      _, b, pr = units[i]
      r0 = b * tm + RW_CHUNK * c
      y_sc[r0:r0 + RW_CHUNK, LANES * pr:LANES * (pr + 1)] = y[:RW_CHUNK] + y[RW_CHUNK:]
      states[i % n_states] = states[i % n_states] * dec[i] + upd
  for i in range(n_states):
    state_sc[i] = states[i]

  y = y_sc[...]
  inv_n = 1.0 / RW_HEAD
  mean = _dot_exact_rhs(y, seg) * inv_n
  d = y - mean
  var = _dot_exact_rhs(d * d, seg) * inv_n
  yn = d * lax.rsqrt(var + RW_LN_EPS) * lng_ref[...] + lnb_ref[...]
  bonus = _dot_exact_rhs(r * km * rk_ref[...], seg) * v
  y_ref[...] = ((yn + bonus) * gate).astype(BF16).reshape(nb, tm, RW_DIM)


def _rwkv(prw, batch, seq, mu, w0, w2, a0, a2, g2, k_k, k_a, r_k, ln_g, ln_b):
  tm = min(RW_TILE, seq)
  rows = batch * tm
  zeros = jnp.zeros((RW_DECAY_LORA, RW_DIM), F32)
  w2e = jnp.concatenate([w2, zeros], axis=0).astype(BF16)
  a2e = jnp.concatenate([zeros, a2], axis=0).astype(BF16)
  idx = jnp.arange(RW_DIM) // RW_HEAD
  seg = (idx[:, None] == idx[None, :]).astype(BF16)
  tok = jnp.arange(rows)
  same_chunk = (tok[:, None] // RW_CHUNK) == (tok[None, :] // RW_CHUNK)
  cum = (same_chunk & (tok[None, :] <= tok[:, None])).astype(BF16)
  allc = same_chunk.astype(BF16)
  row = lambda x: x.reshape(1, -1)
  args = (prw.reshape(batch, seq, RW_IN), row(mu), row(w0), w2e, row(a0), a2e, g2.astype(BF16),
          row(k_k), row(k_a), row(r_k), row(ln_g), row(ln_b), seg, cum, allc)
  in_specs = [pl.BlockSpec((batch, tm, RW_IN), lambda i: (0, i, 0))]
  in_specs += [_resident(a) for a in args[1:]]
  out = pl.pallas_call(
      functools.partial(_rwkv_kernel, tm=tm, nb=batch),
      grid=(seq // tm,),
      in_specs=in_specs,
      out_specs=pl.BlockSpec((batch, tm, RW_DIM), lambda i: (0, i, 0)),
      out_shape=jax.ShapeDtypeStruct((batch, seq, RW_DIM), BF16),
      scratch_shapes=[pltpu.VMEM((8 * batch, RW_IN), F32),
                      pltpu.VMEM((batch * (RW_HEADS // 2), LANES, LANES), F32),
                      pltpu.VMEM((rows, RW_DIM), F32)],
      compiler_params=_cparams("arbitrary"),
      name="rwkv",
  )(*args)
  return out.reshape(batch * seq, RW_DIM)


def _memkv_kernel(mem_ref, g_ref, w_ref, kg_ref, k_ref, v_ref):
  xn = _rms(mem_ref[...], g_ref[...]).astype(BF16)
  kv = jnp.dot(xn, w_ref[...], preferred_element_type=F32)
  for hh in range(X_HEADS):
    sl = slice(X_HEAD * hh, X_HEAD * (hh + 1))
    k_ref[:, sl] = _rms(kv[:, sl], kg_ref[...]).astype(BF16)
  v_ref[...] = kv[:, X_DIM:].astype(BF16)


def _memkv(mem2, batch, g, wkv, k_hnorm):
  args = (mem2, g.reshape(1, -1), wkv.astype(BF16), k_hnorm.reshape(1, -1))
  return pl.pallas_call(
      _memkv_kernel,
      grid=(batch,),
      in_specs=[_rows(MEM_LEN, D_MODEL)] + [_resident(a) for a in args[1:]],
      out_specs=[_rows(MEM_LEN, X_DIM)] * 2,
      out_shape=[jax.ShapeDtypeStruct((batch * MEM_LEN, X_DIM), BF16)] * 2,
      compiler_params=_cparams("parallel"),
      name="memkv",
  )(*args)


def _xattn_block(h, k_ref, v_ref, g_ref, wq_ref, qg_ref, wo_ref):
  xn = _rms(h, g_ref[...]).astype(BF16)
  q = jnp.dot(xn, wq_ref[...], preferred_element_type=F32)
  scale = X_HEAD ** -0.5
  outs = []
  for hh in range(X_HEADS):
    sl = slice(X_HEAD * hh, X_HEAD * (hh + 1))
    qh = _rms(q[:, sl], qg_ref[...]) * scale
    s = _dot_nt(qh, k_ref[:, sl])
    e = jnp.exp(s - jnp.max(s, axis=-1, keepdims=True))
    p = e / jnp.sum(e, axis=-1, keepdims=True)
    outs.append(jnp.dot(p.astype(BF16), v_ref[:, sl], preferred_element_type=F32))
  att = jnp.concatenate(outs, axis=-1).astype(BF16)
  return h + jnp.dot(att, wo_ref[...], preferred_element_type=F32)


FFN_CHUNK = 256


def _ffn_block(h, g_ref, w1_ref, w3_ref, w2_ref):
  xn = _rms(h, g_ref[...]).astype(BF16)
  acc = h
  for c in range(FFN_HIDDEN // FFN_CHUNK):
    sl = slice(FFN_CHUNK * c, FFN_CHUNK * (c + 1))
    u1 = jnp.dot(xn, w1_ref[:, sl], preferred_element_type=F32)
    u3 = jnp.dot(xn, w3_ref[:, sl], preferred_element_type=F32)
    act = (_silu(u1) * u3).astype(BF16)
    acc = acc + jnp.dot(act, w2_ref[sl, :], preferred_element_type=F32)
  return acc


def _tail_kernel(*refs, n):
  h_ref, y_refs, (k_ref, v_ref), w_refs = refs[0], refs[1:1 + n], refs[1 + n:3 + n], refs[3 + n:3 + 2 * n]
  xa_g, wq, qg, wo, ffn_g, w1, w3, w2, o_ref = refs[3 + 2 * n:]
  h = h_ref[...]
  for y_ref, w_ref in zip(y_refs, w_refs):
    h = h + jnp.dot(y_ref[...], w_ref[...], preferred_element_type=F32)
  h = _xattn_block(h, k_ref, v_ref, xa_g, wq, qg, wo)
  o_ref[...] = _ffn_block(h, ffn_g, w1, w3, w2)


def _tail(h, ys, ws, kmem, vmem, seq, xa_g, xa_wq, xa_q_hnorm, xa_wo, ffn_g, ffn_w13, ffn_w2):
  t = h.shape[0]
  tm = min(TOKEN_TILE, seq)
  per_batch = seq // tm
  n = len(ys)
  consts = (xa_g.reshape(1, -1), xa_wq.astype(BF16), xa_q_hnorm.reshape(1, -1), xa_wo.astype(BF16),
            ffn_g.reshape(1, -1), ffn_w13[:, :FFN_HIDDEN].astype(BF16),
            ffn_w13[:, FFN_HIDDEN:].astype(BF16), ffn_w2.astype(BF16))
  mem_spec = pl.BlockSpec((MEM_LEN, X_DIM), lambda i: (i // per_batch, 0))
  in_specs = ([_rows(tm, D_MODEL)] + [_rows(tm, y.shape[1]) for y in ys] + [mem_spec, mem_spec] +
              [_resident(a) for a in (*ws, *consts)])
  return pl.pallas_call(
      functools.partial(_tail_kernel, n=n),
      grid=(t // tm,),
      in_specs=in_specs,
      out_specs=_rows(tm, D_MODEL),
      out_shape=jax.ShapeDtypeStruct((t, D_MODEL), F32),
      compiler_params=_cparams("parallel"),
      name="tail",
  )(h, *ys, kmem, vmem, *ws, *consts)


def _odd_in_kernel(h_ref, g_ref, wz_ref, wx_ref, wdt_ref, cw_ref, cb_ref, z_ref, xbc_ref, dt_ref,
                   xbuf_sc, *, tm, tiles_per_seq):
  @pl.when(pl.program_id(0) % tiles_per_seq == 0)
  def _():
    xbuf_sc[0:8, :] = jnp.zeros((8, SSM_CONV_DIM), F32)

  xn = _rms(h_ref[...], g_ref[...]).astype(BF16)
  dt_ref[...] = jnp.dot(xn, wdt_ref[...], preferred_element_type=F32)
  n_blocks = SSM_CONV_DIM // CONV_COLS

  def project(c):
    cs = slice(CONV_COLS * c, CONV_COLS * (c + 1))
    xbuf_sc[8:8 + tm, cs] = jnp.dot(xn, wx_ref[:, cs], preferred_element_type=F32)

  project(0)
  for c in range(n_blocks):
    cs = slice(CONV_COLS * c, CONV_COLS * (c + 1))
    if c + 1 < n_blocks:
      project(c + 1)
    xe = xbuf_sc[:, cs]
    acc = cw_ref[0:1, cs] * xe
    for j in range(1, SSM_CONV):
      acc = cw_ref[j:j + 1, cs] * xe + pltpu.roll(acc, 1, 0)
    xbuf_sc[0:8, cs] = xe[tm:tm + 8, :]
    xbc_ref[:, cs] = _silu(acc[8:, :] + cb_ref[:, cs]).astype(BF16)
    zs = slice(CONV_COLS * c, min(CONV_COLS * (c + 1), SSM_INNER))
    if zs.start < SSM_INNER:
      z_ref[:, zs] = jnp.dot(xn, wz_ref[:, zs], preferred_element_type=F32).astype(BF16)


def _odd_in(h, seq, g, w_in, conv_w, conv_b):
  t = h.shape[0]
  tm = min(TOKEN_TILE, seq)
  wz = w_in[:, :SSM_INNER].astype(BF16)
  wx = w_in[:, SSM_INNER:SSM_INNER + SSM_CONV_DIM].astype(BF16)
  wdt = w_in[:, SSM_INNER + SSM_CONV_DIM:]
  wdt = jnp.concatenate([wdt, jnp.zeros((D_MODEL, LANES - SSM_HEADS), F32)], axis=1).astype(BF16)
  args = (h, g.reshape(1, -1), wz, wx, wdt, conv_w, conv_b.reshape(1, -1))
  return pl.pallas_call(
      functools.partial(_odd_in_kernel, tm=tm, tiles_per_seq=seq // tm),
      grid=(t // tm,),
      in_specs=[_rows(tm, D_MODEL)] + [_resident(a) for a in args[1:]],
      out_specs=[_rows(tm, SSM_INNER), _rows(tm, SSM_CONV_DIM), _rows(tm, LANES)],
      out_shape=[jax.ShapeDtypeStruct((t, SSM_INNER), BF16),
                 jax.ShapeDtypeStruct((t, SSM_CONV_DIM), BF16),
                 jax.ShapeDtypeStruct((t, LANES), F32)],
      scratch_shapes=[pltpu.VMEM((tm + 8, SSM_CONV_DIM), F32)],
      compiler_params=_cparams("arbitrary"),
      name="odd_in",
  )(*args)


def _ssd_kernel(xbc_ref, z_ref, dt_ref, dtb_ref, alog_ref, dskip_ref, gn_ref, exp_ref, tri_ref,
                y_ref, state_sc, yacc_sc, *, L):
  @pl.when(pl.program_id(1) == 0)
  def _():
    state_sc[...] = jnp.zeros_like(state_sc)

  gn_w = SSM_GROUPS * SSM_STATE
  x = xbc_ref[:, :SSM_INNER]
  b_all = xbc_ref[:, SSM_INNER:SSM_INNER + gn_w]
  c_all = xbc_ref[:, SSM_INNER + gn_w:]

  dt = _softplus(dt_ref[...] + dtb_ref[...])
  a2 = dt * (-jnp.exp(alog_ref[...]) * LOG2E)
  cum = _dot_exact_lhs(tri_ref[...], a2)
  last = cum[L - 1:L, :]
  key_t = (cum - jnp.log2(dt)).T
  expand = exp_ref[...]
  ecum_x = _dot(jnp.exp2(cum), expand)
  wend_x = _dot(dt * jnp.exp2(last - cum), expand)
  elast_x = _dot_exact_rhs(jnp.broadcast_to(jnp.exp2(last), (8, LANES)), expand)[0:1, :]
  xend = (x * wend_x).astype(BF16)

  row = lax.broadcasted_iota(jnp.int32, (L, L), 0)
  col = lax.broadcasted_iota(jnp.int32, (L, L), 1)
  causal = col <= row
  lane = lax.broadcasted_iota(jnp.int32, (1, LANES), 1)
  first = lane < SSM_HEAD
  zero = jnp.zeros((), BF16)

  e_per = SSM_HEADS // SSM_GROUPS
  for g in range(SSM_GROUPS):
    gs = slice(SSM_GROUP_W * g, SSM_GROUP_W * (g + 1))
    bg = b_all[:, SSM_STATE * g:SSM_STATE * (g + 1)]
    cg = c_all[:, SSM_STATE * g:SSM_STATE * (g + 1)]
    cb = _dot_nt(cg, bg)
    st = state_sc[:, gs]
    yacc_sc[:, gs] = _dot(cg, st) * ecum_x[:, gs]
    for pr in range(e_per // 2):
      ps = slice(SSM_GROUP_W * g + LANES * pr, SSM_GROUP_W * g + LANES * (pr + 1))
      xp = x[:, ps]
      y_pair = jnp.zeros((L, LANES), F32)
      for j in range(2):
        hd = e_per * g + 2 * pr + j
        decay = jnp.exp2(cum[:, hd:hd + 1] - key_t[hd:hd + 1, :])
        m = jnp.where(causal, cb * decay, 0.0).astype(BF16)
        xj = jnp.where(first, xp, zero) if j == 0 else jnp.where(first, zero, xp)
        y_pair = y_pair + jnp.dot(m, xj, preferred_element_type=F32)
      yacc_sc[:, ps] += y_pair
    state_sc[:, gs] = st * elast_x[:, gs] + _dot_tn(bg, xend[:, gs])

  y = yacc_sc[...] + x * dskip_ref[...]
  y = y * _silu(z_ref[...].astype(F32))
  for g in range(SSM_GROUPS):
    gs = slice(SSM_GROUP_W * g, SSM_GROUP_W * (g + 1))
    y_ref[:, gs] = _rms(y[:, gs], gn_ref[:, gs]).astype(BF16)


def _ssd(z, xbc, dt, batch, seq, dt_bias, a_log, d_skip, gnorm):
  L = min(SSM_CHUNK, seq)
  nc = seq // L
  pad = lambda x: jnp.concatenate([x, jnp.zeros((LANES - SSM_HEADS,), F32)]).reshape(1, LANES)
  head_of_lane = jnp.arange(SSM_INNER) // SSM_HEAD
  expand = (jnp.arange(LANES)[:, None] == head_of_lane[None, :]).astype(BF16)
  tok = jnp.arange(L)
  tri = (tok[None, :] <= tok[:, None]).astype(BF16)
  args = (xbc, z, dt, pad(dt_bias), pad(a_log), jnp.repeat(d_skip, SSM_HEAD).reshape(1, -1),
          gnorm.reshape(1, -1), expand, tri)
  tiles = lambda w: pl.BlockSpec((L, w), lambda b, i: (b * nc + i, 0))
  return pl.pallas_call(
      functools.partial(_ssd_kernel, L=L),
      grid=(batch, nc),
      in_specs=[tiles(SSM_CONV_DIM), tiles(SSM_INNER), tiles(LANES)] + [_resident(a) for a in args[3:]],
      out_specs=tiles(SSM_INNER),
      out_shape=jax.ShapeDtypeStruct((batch * seq, SSM_INNER), BF16),
      scratch_shapes=[pltpu.VMEM((SSM_STATE, SSM_INNER), F32),
                      pltpu.VMEM((L, SSM_INNER), F32)],
      compiler_params=_cparams("parallel", "arbitrary"),
      name="ssd",
  )(*args)


def kernel(x, mem, positions, ev_norm, ev_w_in, mla_q_norm, mla_w_uq, mla_kv_norm, mla_w_ukv, mla_q_hnorm, mla_k_hnorm, rw_mu, rw_w0, rw_w2, rw_a0, rw_a2, rw_g2, rw_k_k, rw_k_a, rw_r_k, rw_ln_g, rw_ln_b, ev_w_out, od_norm, od_w_in, ssm_conv_w, ssm_conv_b, ssm_dt_bias, ssm_a_log, ssm_d, ssm_gnorm, od_w_out, xa_norm_x, xa_norm_mem, xa_wq, xa_wkv, xa_q_hnorm, xa_k_hnorm, xa_wo, ffn_norm, ffn_w13, ffn_w2):
  batch, seq, _ = x.shape
  depth = xa_wq.shape[0]
  h = x.reshape(batch * seq, D_MODEL)
  mem2 = mem.reshape(batch * MEM_LEN, D_MODEL)
  pos = positions.reshape(batch * seq)
  for i in range(depth):
    j = i // 2
    kmem, vmem = _memkv(mem2, batch, xa_norm_mem[i], xa_wkv[i], xa_k_hnorm[i])
    if i % 2 == 0:
      q, k, v, prw = _even_in(h, pos, ev_norm[j], ev_w_in[j], mla_q_norm[j], mla_w_uq[j],
                              mla_kv_norm[j], mla_w_ukv[j], mla_q_hnorm[j], mla_k_hnorm[j])
      y_mla = _flash(q, k, v, batch, seq)
      y_rw = _rwkv(prw, batch, seq, rw_mu[j], rw_w0[j], rw_w2[j], rw_a0[j], rw_a2[j], rw_g2[j],
                   rw_k_k[j], rw_k_a[j], rw_r_k[j].reshape(-1), rw_ln_g[j], rw_ln_b[j])
      w_out = ev_w_out[j].astype(BF16)
      ys, ws = [y_mla, y_rw], [w_out[:MLA_HEADS * MLA_V], w_out[MLA_HEADS * MLA_V:]]
    else:
      z, xbc, dt = _odd_in(h, seq, od_norm[j], od_w_in[j], ssm_conv_w[j], ssm_conv_b[j])
      y = _ssd(z, xbc, dt, batch, seq, ssm_dt_bias[j], ssm_a_log[j], ssm_d[j], ssm_gnorm[j])
      ys, ws = [y], [od_w_out[j].astype(BF16)]
    h = _tail(h, ys, ws, kmem, vmem, seq, xa_norm_x[i], xa_wq[i], xa_q_hnorm[i], xa_wo[i],
              ffn_norm[i], ffn_w13[i], ffn_w2[i])
  return h.reshape(batch, seq, D_MODEL)
```
